```python
import jax
import jax.numpy as jnp
from jax import lax
import numpy as np

D_MODEL = 4096
BATCH = 4
SEQ = 4096
DEPTH = 4

CHUNK = 64
N_MIXERS = 3
NORM_EPS = 1e-6

SB_HEADS = 32
SB_HEAD_DIM = D_MODEL // SB_HEADS
Q_BLOCK = 128

RWKV_HEAD_DIM = 64
RWKV_HEADS = D_MODEL // RWKV_HEAD_DIM
DECAY_LORA = max(32, int(round(1.8 * D_MODEL ** 0.5 / 32)) * 32)
AAA_LORA = max(32, int(round(1.8 * D_MODEL ** 0.5 / 32)) * 32)
GATE_LORA = max(32, int(round(0.6 * D_MODEL ** 0.8 / 32)) * 32)
RWKV_GN_EPS = 64e-5
N_TOKEN_SHIFT = 6

CA_HEADS = 32
CA_HEAD_DIM = D_MODEL // CA_HEADS
LEFT_CHUNKS = 8
BAND = (LEFT_CHUNKS + 1) * CHUNK
REL_CLIP = 256

FFN_DIM = 2 * D_MODEL
N_EXPERTS = 8
TOP_K = 2
EXPERT_DIM = 7 * D_MODEL // 16

N_SB_LAYERS = (DEPTH + 2) // 3
N_RWKV_LAYERS = (DEPTH + 1) // 3
N_CA_LAYERS = DEPTH // 3
N_DENSE_LAYERS = (DEPTH + 1) // 2
N_MOE_LAYERS = DEPTH // 2

kernel_name = 'hybrid_stream_encoder_block'


def rms_norm(x, gain, eps=NORM_EPS):
    x32 = x.astype(jnp.float32)
    y = x32 * lax.rsqrt(jnp.mean(x32 * x32, axis=-1, keepdims=True) + eps)
    return (y * gain.astype(jnp.float32)).astype(x.dtype)


def stick_breaking_attention(xn, w_qkv, w_o):
    B, T, _ = xn.shape
    qkv = (xn @ w_qkv).reshape(B, T, 3, SB_HEADS, SB_HEAD_DIM)
    q, k, v = (jnp.moveaxis(qkv[:, :, i], 1, 2) for i in range(3))
    scale = SB_HEAD_DIM ** -0.5
    outs = []
    for t0 in range(0, T, Q_BLOCK):
        t1 = t0 + Q_BLOCK
        z = jnp.einsum('bhqd,bhkd->bhqk', q[:, :, t0:t1], k[:, :, :t1]).astype(jnp.float32) * scale
        past = jnp.arange(t1)[None, :] < (t0 + jnp.arange(Q_BLOCK))[:, None]
        log_keep = jnp.where(past, jax.nn.log_sigmoid(-z), 0.0)
        log_w = jax.nn.log_sigmoid(z) + lax.cumsum(log_keep, axis=3, reverse=True) - log_keep
        att = jnp.where(past, jnp.exp(log_w), 0.0)
        outs.append(jnp.einsum('bhqk,bhkd->bqhd', att.astype(v.dtype), v[:, :, :t1]))
    o = jnp.concatenate(outs, axis=1).reshape(B, T, D_MODEL)
    return o @ w_o


def rwkv7_time_mix(xn, mu, w_rkv, w0, w1, w2, a0, a1, a2, g1, g2, k_k, k_a, r_k, ln_w, ln_b, w_o):
    B, T, D = xn.shape
    H, N = RWKV_HEADS, RWKV_HEAD_DIM
    f32 = jnp.float32
    dx = jnp.pad(xn, ((0, 0), (1, 0), (0, 0)))[:, :-1] - xn

    def mix(i):
        return xn + dx * mu[i]

    r = mix(0) @ w_rkv[0]
    k = mix(2) @ w_rkv[1]
    v = mix(3) @ w_rkv[2]
    w_log = -jax.nn.softplus(-(w0 + jnp.tanh(mix(1) @ w1) @ w2).astype(f32)) - 0.5
    decay = jnp.exp(-jnp.exp(w_log))
    a = jax.nn.sigmoid((a0 + (mix(4) @ a1) @ a2).astype(f32))
    g = jax.nn.sigmoid(mix(5) @ g1) @ g2

    def heads(t):
        return t.astype(f32).reshape(B, T, H, N)

    r, k, v, decay, a = heads(r), heads(k), heads(v), heads(decay), heads(a)
    kk = k * k_k.astype(f32).reshape(H, N)
    kk = kk * lax.rsqrt(jnp.maximum(jnp.sum(kk * kk, axis=-1, keepdims=True), 1e-24))
    k = k * (1.0 + (a - 1.0) * k_a.astype(f32).reshape(H, N))

    def step(S, inp):
        r_t, w_t, k_t, v_t, kk_t, a_t = inp
        sa = jnp.einsum('bhvk,bhk->bhv', S, -kk_t)
        S = (S * w_t[:, :, None, :] + sa[..., None] * (kk_t * a_t)[:, :, None, :]
             + v_t[..., None] * k_t[:, :, None, :])
        return S, jnp.einsum('bhvk,bhk->bhv', S, r_t)

    xs = tuple(jnp.moveaxis(t, 1, 0) for t in (r, decay, k, v, kk, a))
    _, y = lax.scan(step, jnp.zeros((B, H, N, N), f32), xs)
    y = jnp.moveaxis(y, 0, 1)
    mean = jnp.mean(y, axis=-1, keepdims=True)
    var = jnp.mean(jnp.square(y - mean), axis=-1, keepdims=True)
    y = ((y - mean) * lax.rsqrt(var + RWKV_GN_EPS)).reshape(B, T, D) * ln_w.astype(f32) + ln_b.astype(f32)
    bonus = jnp.sum(r * k * r_k.astype(f32), axis=-1, keepdims=True) * v
    out = (y + bonus.reshape(B, T, D)) * g.astype(f32)
    return out.astype(xn.dtype) @ w_o


def chunk_band_attention(xn, w_qkv, q_norm, k_norm, rel_bias, w_o):
    B, T, _ = xn.shape
    f32 = jnp.float32
    qkv = (xn @ w_qkv).reshape(B, T, 3, CA_HEADS, CA_HEAD_DIM)
    q = rms_norm(qkv[:, :, 0], q_norm)
    k = rms_norm(qkv[:, :, 1], k_norm)
    v = qkv[:, :, 2]
    left = LEFT_CHUNKS * CHUNK
    k_pad = jnp.pad(k, ((0, 0), (left, 0), (0, 0), (0, 0)))
    v_pad = jnp.pad(v, ((0, 0), (left, 0), (0, 0), (0, 0)))
    rel = jnp.arange(CHUNK)[:, None] + left - jnp.arange(BAND)[None, :]
    bias = rel_bias[:, jnp.clip(rel, -REL_CLIP, REL_CLIP) + REL_CLIP].astype(f32)
    scale = CA_HEAD_DIM ** -0.5

    def one_chunk(c):
        start = c * CHUNK
        q_c = lax.dynamic_slice_in_dim(q, start, CHUNK, axis=1)
        k_c = lax.dynamic_slice_in_dim(k_pad, start, BAND, axis=1)
        v_c = lax.dynamic_slice_in_dim(v_pad, start, BAND, axis=1)
        s = jnp.einsum('bqhd,bkhd->bhqk', q_c, k_c).astype(f32) * scale + bias
        valid = start - left + jnp.arange(BAND) >= 0
        p = jax.nn.softmax(jnp.where(valid, s, -jnp.inf), axis=-1)
        return jnp.einsum('bhqk,bkhd->bqhd', p.astype(v_c.dtype), v_c)

    o = lax.map(one_chunk, jnp.arange(T // CHUNK))
    o = jnp.moveaxis(o, 0, 1).reshape(B, T, D_MODEL)
    return o @ w_o


def swiglu(xn, w_gate_up, w_down):
    g, u = jnp.split(xn @ w_gate_up, 2, axis=-1)
    return (jax.nn.silu(g) * u) @ w_down


def moe_swiglu(xn, router, w_gate_up, w_down):
    B, T, D = xn.shape
    xt = xn.reshape(B * T, D)
    probs = jax.nn.softmax((xt @ router).astype(jnp.float32), axis=-1)
    top_p, top_i = lax.top_k(probs, TOP_K)
    top_p = top_p / jnp.sum(top_p, axis=-1, keepdims=True)
    gates = jnp.sum(jax.nn.one_hot(top_i, N_EXPERTS, dtype=jnp.float32) * top_p[..., None], axis=1)
    y = jnp.zeros_like(xt)
    for e in range(N_EXPERTS):
        y = y + gates[:, e:e + 1].astype(xt.dtype) * swiglu(xt, w_gate_up[e], w_down[e])
    return y.reshape(B, T, D)


def setup_inputs(seed: int = 0) -> dict:
    key = jax.random.key(seed)
    ks = jax.random.split(key, 31)
    f32 = jnp.float32
    D = D_MODEL

    def nrm(i, shape, scale):
        return scale * jax.random.normal(ks[i], shape, f32)

    def uni(i, shape, lo, hi):
        return jax.random.uniform(ks[i], shape, f32, lo, hi)

    return {
        'x': nrm(0, (BATCH, SEQ, D), 1.0),
        'mix_norm': 1.0 + nrm(1, (DEPTH, D), 0.02),
        'ffn_norm': 1.0 + nrm(2, (DEPTH, D), 0.02),
        'sb_w_qkv': nrm(3, (N_SB_LAYERS, D, 3 * D), D ** -0.5),
        'sb_w_o': nrm(4, (N_SB_LAYERS, D, D), D ** -0.5),
        'rwkv_mu': uni(5, (N_RWKV_LAYERS, N_TOKEN_SHIFT, D), 0.0, 1.0),
        'rwkv_w_rkv': nrm(6, (N_RWKV_LAYERS, 3, D, D), D ** -0.5),
        'rwkv_w0': uni(7, (N_RWKV_LAYERS, D), -6.0, -1.0),
        'rwkv_w1': nrm(8, (N_RWKV_LAYERS, D, DECAY_LORA), D ** -0.5),
        'rwkv_w2': nrm(9, (N_RWKV_LAYERS, DECAY_LORA, D), 0.1 * DECAY_LORA ** -0.5),
        'rwkv_a0': nrm(10, (N_RWKV_LAYERS, D), 0.1),
        'rwkv_a1': nrm(11, (N_RWKV_LAYERS, D, AAA_LORA), D ** -0.5),
        'rwkv_a2': nrm(12, (N_RWKV_LAYERS, AAA_LORA, D), 0.5 * AAA_LORA ** -0.5),
        'rwkv_g1': nrm(13, (N_RWKV_LAYERS, D, GATE_LORA), D ** -0.5),
        'rwkv_g2': nrm(14, (N_RWKV_LAYERS, GATE_LORA, D), GATE_LORA ** -0.5),
        'rwkv_k_k': 0.85 + nrm(15, (N_RWKV_LAYERS, D), 0.02),
        'rwkv_k_a': 1.0 + nrm(16, (N_RWKV_LAYERS, D), 0.02),
        'rwkv_r_k': nrm(17, (N_RWKV_LAYERS, RWKV_HEADS, RWKV_HEAD_DIM), 0.1),
        'rwkv_ln_w': 1.0 + nrm(18, (N_RWKV_LAYERS, D), 0.02),
        'rwkv_ln_b': nrm(19, (N_RWKV_LAYERS, D), 0.02),
        'rwkv_w_o': nrm(20, (N_RWKV_LAYERS, D, D), D ** -0.5),
        'ca_w_qkv': nrm(21, (N_CA_LAYERS, D, 3 * D), D ** -0.5),
        'ca_q_norm': 1.0 + nrm(22, (N_CA_LAYERS, CA_HEAD_DIM), 0.02),
        'ca_k_norm': 1.0 + nrm(23, (N_CA_LAYERS, CA_HEAD_DIM), 0.02),
        'ca_rel_bias': nrm(24, (N_CA_LAYERS, CA_HEADS, 2 * REL_CLIP + 1), 0.5),
        'ca_w_o': nrm(25, (N_CA_LAYERS, D, D), D ** -0.5),
        'ffn_w_gate_up': nrm(26, (N_DENSE_LAYERS, D, 2 * FFN_DIM), D ** -0.5),
        'ffn_w_down': nrm(27, (N_DENSE_LAYERS, FFN_DIM, D), FFN_DIM ** -0.5),
        'moe_router': nrm(28, (N_MOE_LAYERS, D, N_EXPERTS), D ** -0.5),
        'moe_w_gate_up': nrm(29, (N_MOE_LAYERS, N_EXPERTS, D, 2 * EXPERT_DIM), D ** -0.5),
        'moe_w_down': nrm(30, (N_MOE_LAYERS, N_EXPERTS, EXPERT_DIM, D), EXPERT_DIM ** -0.5),
    }


def reference(x, mix_norm, ffn_norm, sb_w_qkv, sb_w_o, rwkv_mu, rwkv_w_rkv, rwkv_w0, rwkv_w1, rwkv_w2,
              rwkv_a0, rwkv_a1, rwkv_a2, rwkv_g1, rwkv_g2, rwkv_k_k, rwkv_k_a, rwkv_r_k, rwkv_ln_w, rwkv_ln_b,
              rwkv_w_o, ca_w_qkv, ca_q_norm, ca_k_norm, ca_rel_bias, ca_w_o, ffn_w_gate_up, ffn_w_down,
              moe_router, moe_w_gate_up, moe_w_down):
    h = x
    for i in range(DEPTH):
        kind, j = i % N_MIXERS, i // N_MIXERS
        xn = rms_norm(h, mix_norm[i])
        if kind == 0:
            y = stick_breaking_attention(xn, sb_w_qkv[j], sb_w_o[j])
        elif kind == 1:
            y = rwkv7_time_mix(xn, rwkv_mu[j], rwkv_w_rkv[j], rwkv_w0[j], rwkv_w1[j], rwkv_w2[j],
                               rwkv_a0[j], rwkv_a1[j], rwkv_a2[j], rwkv_g1[j], rwkv_g2[j], rwkv_k_k[j],
                               rwkv_k_a[j], rwkv_r_k[j], rwkv_ln_w[j], rwkv_ln_b[j], rwkv_w_o[j])
        else:
            y = chunk_band_attention(xn, ca_w_qkv[j], ca_q_norm[j], ca_k_norm[j], ca_rel_bias[j], ca_w_o[j])
        h = h + y
        xn = rms_norm(h, ffn_norm[i])
        f = i // 2
        if i % 2 == 0:
            h = h + swiglu(xn, ffn_w_gate_up[f], ffn_w_down[f])
        else:
            h = h + moe_swiglu(xn, moe_router[f], moe_w_gate_up[f], moe_w_down[f])
    return h
```

```python
import functools

import jax
import jax.numpy as jnp
from jax import lax
from jax.experimental import pallas as pl
from jax.experimental.pallas import tpu as pltpu

F32 = jnp.float32
BF16 = jnp.bfloat16

NORM_EPS = 1e-6
SB_HEADS = 32
RWKV_HEAD_DIM = 64
RWKV_GN_EPS = 64e-5
CA_HEADS = 32
CHUNK = 64
LEFT_CHUNKS = 8
REL_CLIP = 256
TOP_K = 2
N_MIXERS = 3

V7X_LANES = 128
V7X_MXU_DIM = 256
V7X_VMEM_BYTES = 64 * 1024 * 1024
V7X_VMEM_CAP = 56 * 1024 * 1024

MM_TM = 1024
MM_TN = 1024
MM_TK = 2048
ROW_TILE = 256
SB_TILE = 256
CA_TILE = 256
RWKV_L = 64
RWKV_G = 4
RWKV_TB = 256


def _tile(dim, pref, align=V7X_LANES):
    if dim <= pref:
        return dim
    t = (pref // align) * align
    while t >= align:
        if dim % t == 0:
            return t
        t -= align
    return dim


def _params(sem, vmem_bytes):
    limit = int(min(max(vmem_bytes * 5 // 4 + (4 << 20), 16 << 20), V7X_VMEM_CAP))
    return pltpu.CompilerParams(dimension_semantics=sem, vmem_limit_bytes=limit)


def _rms(x, gain):
    return x * lax.rsqrt(jnp.mean(x * x, axis=-1, keepdims=True) + NORM_EPS) * gain


def _rmsnorm_kernel(h_ref, g_ref, o_ref):
    o_ref[...] = _rms(h_ref[...], g_ref[...]).astype(o_ref.dtype)


def rmsnorm(h, gain, out_dtype=BF16):
    n, d = h.shape
    tr = _tile(n, ROW_TILE, 8)
    return pl.pallas_call(
        _rmsnorm_kernel,
        grid=(n // tr,),
        in_specs=[pl.BlockSpec((tr, d), lambda i: (i, 0)),
                  pl.BlockSpec((1, d), lambda i: (0, 0))],
        out_specs=pl.BlockSpec((tr, d), lambda i: (i, 0)),
        out_shape=jax.ShapeDtypeStruct((n, d), out_dtype),
        compiler_params=_params(("parallel",), 2 * tr * d * 6),
        name="rmsnorm",
    )(h, gain.reshape(1, d))


def _rmsnorm_shift_kernel(h_ref, hp_ref, g_ref, mu_ref, *o_refs, tiles_per_seq):
    i = pl.program_id(0)
    g = g_ref[...]
    xn = _rms(h_ref[...], g)
    prev_last = _rms(hp_ref[...], g)[7:8, :]
    prev_last = jnp.where(i % tiles_per_seq == 0, 0.0, prev_last)
    rows = lax.broadcasted_iota(jnp.int32, xn.shape, 0)
    shifted = jnp.where(rows == 0, prev_last, pltpu.roll(xn, 1, axis=0))
    dx = shifted - xn
    for m, o_ref in enumerate(o_refs):
        o_ref[...] = (xn + dx * mu_ref[m:m + 1, :]).astype(o_ref.dtype)


def rmsnorm_shift(h, gain, mu, seq):
    n, d = h.shape
    nm = mu.shape[0]
    tr = _tile(seq, ROW_TILE, 8)
    tps = seq // tr
    sub = tr // 8
    return pl.pallas_call(
        functools.partial(_rmsnorm_shift_kernel, tiles_per_seq=tps),
        grid=(n // tr,),
        in_specs=[pl.BlockSpec((tr, d), lambda i: (i, 0)),
                  pl.BlockSpec((8, d), lambda i: (jnp.maximum(i * sub - 1, 0), 0)),
                  pl.BlockSpec((1, d), lambda i: (0, 0)),
                  pl.BlockSpec((nm, d), lambda i: (0, 0))],
        out_specs=[pl.BlockSpec((tr, d), lambda i: (i, 0))] * nm,
        out_shape=[jax.ShapeDtypeStruct((n, d), BF16)] * nm,
        compiler_params=_params(("parallel",), 2 * tr * d * (4 + 2 * nm) + 4 * tr * d * 4),
        name="rmsnorm_shift",
    )(h, h, gain.reshape(1, d), mu)


def _mm_kernel(*refs, n_extra, epi, nk):
    a_ref, b_ref = refs[0], refs[1]
    extra = refs[2:2 + n_extra]
    o_ref = refs[2 + n_extra]

    def finish(acc):
        o_ref[...] = epi(acc, *[e[...] for e in extra]).astype(o_ref.dtype)

    part = jnp.dot(a_ref[...], b_ref[...], preferred_element_type=F32)
    if nk == 1:
        finish(part)
        return
    acc_ref = refs[3 + n_extra]
    k = pl.program_id(2)

    @pl.when(k == 0)
    def _():
        acc_ref[...] = part

    @pl.when(jnp.logical_and(k > 0, k < nk - 1))
    def _():
        acc_ref[...] += part

    @pl.when(k == nk - 1)
    def _():
        finish(acc_ref[...] + part)


def _epi_id(acc):
    return acc


def matmul(a, b, out_dtype, epi=_epi_id, cols=(), fulls=(), tm=MM_TM, tn=MM_TN, tk=MM_TK, name="matmul"):
    m, kd = a.shape
    _, n = b.shape
    tm, tn = _tile(m, tm, 8), _tile(n, tn)
    tk = kd if kd <= 2 * tk else _tile(kd, tk)
    nk = kd // tk
    extras = [c.reshape(1, n).astype(F32) for c in cols] + list(fulls)
    in_specs = [pl.BlockSpec((tm, tk), lambda i, j, k: (i, k)),
                pl.BlockSpec((tk, tn), lambda i, j, k: (k, j))]
    in_specs += [pl.BlockSpec((1, tn), lambda i, j, k: (0, j)) for _ in cols]
    in_specs += [pl.BlockSpec((tm, tn), lambda i, j, k: (i, j)) for _ in fulls]
    osize = jnp.dtype(out_dtype).itemsize
    vmem = 2 * (tm * tk + tk * tn) * a.dtype.itemsize + 2 * tm * tn * osize
    vmem += sum(2 * tm * tn * f.dtype.itemsize for f in fulls) + (tm * tn * 4 if nk > 1 else 0)
    vmem += tm * tn * 4
    return pl.pallas_call(
        functools.partial(_mm_kernel, n_extra=len(extras), epi=epi, nk=nk),
        grid=(m // tm, n // tn, nk),
        in_specs=in_specs,
        out_specs=pl.BlockSpec((tm, tn), lambda i, j, k: (i, j)),
        out_shape=jax.ShapeDtypeStruct((m, n), out_dtype),
        scratch_shapes=[pltpu.VMEM((tm, tn), F32)] if nk > 1 else [],
        compiler_params=_params(("parallel", "parallel", "arbitrary"), vmem),
        name=name,
    )(a, b, *extras)


def _epi_residual(acc, res):
    return res + acc


def _silu(x):
    return x * (1.0 / (1.0 + jnp.exp(-x)))


def _glu_kernel(*refs, nk, n_exp, gated):
    if gated:
        a_ref, bg_ref, bu_ref, gate_ref, o_ref = refs[:5]
        scratch = refs[5:]
    else:
        a_ref, bg_ref, bu_ref, o_ref = refs[:4]
        gate_ref = None
        scratch = refs[4:]

    def finish(g, u):
        y = _silu(g) * u
        if gated:
            e = pl.program_id(1)
            gates = gate_ref[...]
            lane = lax.broadcasted_iota(jnp.int32, gates.shape, 1)
            y = y * jnp.sum(jnp.where(lane == e, gates, 0.0), axis=1, keepdims=True)
        o_ref[...] = y.astype(o_ref.dtype)

    a = a_ref[...]
    pg = jnp.dot(a, bg_ref[...], preferred_element_type=F32)
    pu = jnp.dot(a, bu_ref[...], preferred_element_type=F32)
    if nk == 1:
        finish(pg, pu)
        return
    accg_ref, accu_ref = scratch
    k = pl.program_id(3)

    @pl.when(k == 0)
    def _():
        accg_ref[...] = pg
        accu_ref[...] = pu

    @pl.when(jnp.logical_and(k > 0, k < nk - 1))
    def _():
        accg_ref[...] += pg
        accu_ref[...] += pu

    @pl.when(k == nk - 1)
    def _():
        finish(accg_ref[...] + pg, accu_ref[...] + pu)


def glu_matmul(a, w, gates=None, tm=MM_TM, tn=MM_TN // 2, tk=MM_TK, name="glu_matmul"):
    m, kd = a.shape
    n_exp, _, f2 = w.shape
    f = f2 // 2
    tm = _tile(m, tm, 8)
    tn = _tile(f, tn)
    tk = kd if kd <= 2 * tk else _tile(kd, tk)
    nk = kd // tk
    nj = f // tn
    gated = gates is not None
    in_specs = [pl.BlockSpec((tm, tk), lambda i, e, j, k: (i, k)),
                pl.BlockSpec((None, tk, tn), lambda i, e, j, k: (e, k, j)),
                pl.BlockSpec((None, tk, tn), lambda i, e, j, k: (e, k, nj + j))]
    args = [a, w, w]
    if gated:
        in_specs.append(pl.BlockSpec((tm, gates.shape[1]), lambda i, e, j, k: (i, 0)))
        args.append(gates)
    vmem = 2 * (tm * tk + 2 * tk * tn) * 2 + 2 * tm * tn * 2 + 4 * tm * tn * 4 + 2 * tm * 128 * 4
    return pl.pallas_call(
        functools.partial(_glu_kernel, nk=nk, n_exp=n_exp, gated=gated),
        grid=(m // tm, n_exp, nj, nk),
        in_specs=in_specs,
        out_specs=pl.BlockSpec((tm, tn), lambda i, e, j, k: (i, e * nj + j)),
        out_shape=jax.ShapeDtypeStruct((m, n_exp * f), BF16),
        scratch_shapes=[pltpu.VMEM((tm, tn), F32), pltpu.VMEM((tm, tn), F32)] if nk > 1 else [],
        compiler_params=_params(("parallel", "parallel", "parallel", "arbitrary"), vmem),
        name=name,
    )(*args)


def _split2(x):
    hi = x.astype(BF16)
    return hi, (x - hi.astype(F32)).astype(BF16)


def _split3(x):
    hi = x.astype(BF16)
    r1 = x - hi.astype(F32)
    mid = r1.astype(BF16)
    return hi, mid, (r1 - mid.astype(F32)).astype(BF16)


def _dot(a, b):
    return jnp.dot(a, b, preferred_element_type=F32)


def _dot_nt(a, b):
    return lax.dot_general(a, b, (((1,), (1,)), ((), ())), preferred_element_type=F32)


def _dot_tn(a, b):
    return lax.dot_general(a, b, (((0,), (0,)), ((), ())), preferred_element_type=F32)


def _sb_kernel(q_ref, k_ref, v_ref, o_ref, *, scale):
    t = q_ref.shape[0]
    hd = q_ref.shape[1]
    qi = pl.program_id(2)
    q = q_ref[...]
    row = lax.broadcasted_iota(jnp.int32, (t, t), 0)
    col = lax.broadcasted_iota(jnp.int32, (t, t), 1)
    later = (row > col).astype(BF16)
    past = col < row

    def block(kb, carry, diag):
        o, c = carry
        start = pl.multiple_of(kb * t, t)
        z = _dot_nt(q, k_ref[pl.ds(start, t), :]) * scale
        ls = jnp.minimum(z, 0.0) - jnp.log1p(jnp.exp(-jnp.abs(z)))
        lk = ls - z
        if diag:
            lk = jnp.where(past, lk, 0.0)
        hi, lo = _split2(lk)
        att = jnp.exp(ls + (_dot(hi, later) + _dot(lo, later)) + c)
        if diag:
            att = jnp.where(past, att, 0.0)
        o = o + _dot(att.astype(BF16), v_ref[pl.ds(start, t), :])
        return o, c + jnp.sum(lk, axis=1, keepdims=True)

    carry = block(qi, (jnp.zeros((t, hd), F32), jnp.zeros((t, 1), F32)), True)
    o, _ = lax.fori_loop(0, qi, lambda it, cr: block(qi - 1 - it, cr, False), carry)
    o_ref[...] = o.astype(o_ref.dtype)


def sb_attention(qkv, n_heads):
    b, t, d3 = qkv.shape
    d = d3 // 3
    hd = d // n_heads
    tq = _tile(t, SB_TILE, 8)
    return pl.pallas_call(
        functools.partial(_sb_kernel, scale=hd ** -0.5),
        grid=(b, n_heads, t // tq),
        in_specs=[pl.BlockSpec((None, tq, hd), lambda bi, h, qi: (bi, qi, h)),
                  pl.BlockSpec((None, t, hd), lambda bi, h, qi: (bi, 0, n_heads + h)),
                  pl.BlockSpec((None, t, hd), lambda bi, h, qi: (bi, 0, 2 * n_heads + h))],
        out_specs=pl.BlockSpec((None, tq, hd), lambda bi, h, qi: (bi, qi, h)),
        out_shape=jax.ShapeDtypeStruct((b, t, d), BF16),
        compiler_params=_params(("parallel", "parallel", "arbitrary"),
                                4 * t * hd * 2 + 4 * tq * hd * 2 + 24 * tq * tq * 4),
        name="sb_attention",
    )(qkv, qkv, qkv)


NEG_BIG = -1e30


def _ca_kernel(q_ref, k_ref, v_ref, qg_ref, kg_ref, bias_ref, o_ref, kn_ref, *, scale, n_win):
    t = q_ref.shape[0]
    qi = pl.program_id(2)

    @pl.when(qi == 0)
    def _():
        kn_ref[...] = _rms(k_ref[...].astype(F32), kg_ref[...]).astype(BF16)

    qn = _rms(q_ref[...].astype(F32), qg_ref[...]).astype(BF16)
    scores, starts = [], []
    for w in range(n_win):
        kb = qi - (n_win - 1) + w
        start = pl.multiple_of(jnp.maximum(kb, 0) * t, t)
        s = _dot_nt(qn, kn_ref[pl.ds(start, t), :]) * scale + bias_ref[:, w * t:(w + 1) * t]
        scores.append(jnp.where(kb >= 0, s, NEG_BIG))
        starts.append(start)
    m = functools.reduce(jnp.maximum, [jnp.max(s, axis=1, keepdims=True) for s in scores])
    acc = jnp.zeros(o_ref.shape, F32)
    den = jnp.zeros((t, 1), F32)
    for s, start in zip(scores, starts):
        p = jnp.exp(s - m)
        den = den + jnp.sum(p, axis=1, keepdims=True)
        acc = acc + _dot(p.astype(BF16), v_ref[pl.ds(start, t), :])
    o_ref[...] = (acc / den).astype(o_ref.dtype)


def _ca_bias_table(rel_bias, t):
    left = LEFT_CHUNKS * CHUNK
    i = jnp.arange(t)[:, None]
    j = jnp.arange(left + t)[None, :]
    rel = jnp.clip(i + left - j, -REL_CLIP, REL_CLIP) + REL_CLIP
    qc = (i + left) // CHUNK
    kc = j // CHUNK
    allowed = jnp.logical_and(kc <= qc, kc >= qc - LEFT_CHUNKS)
    return jnp.where(allowed[None], rel_bias.astype(F32)[:, rel], NEG_BIG)


def ca_attention(qkv, q_gain, k_gain, rel_bias, n_heads):
    b, t, d3 = qkv.shape
    d = d3 // 3
    hd = d // n_heads
    tq = _tile(t, CA_TILE, CHUNK)
    left = LEFT_CHUNKS * CHUNK
    assert left % tq == 0 and tq % CHUNK == 0
    n_win = left // tq + 1
    bias = _ca_bias_table(rel_bias, tq)
    return pl.pallas_call(
        functools.partial(_ca_kernel, scale=hd ** -0.5, n_win=n_win),
        grid=(b, n_heads, t // tq),
        in_specs=[pl.BlockSpec((None, tq, hd), lambda bi, h, qi: (bi, qi, h)),
                  pl.BlockSpec((None, t, hd), lambda bi, h, qi: (bi, 0, n_heads + h)),
                  pl.BlockSpec((None, t, hd), lambda bi, h, qi: (bi, 0, 2 * n_heads + h)),
                  pl.BlockSpec((1, hd), lambda bi, h, qi: (0, 0)),
                  pl.BlockSpec((1, hd), lambda bi, h, qi: (0, 0)),
                  pl.BlockSpec((None, tq, left + tq), lambda bi, h, qi: (h, 0, 0))],
        out_specs=pl.BlockSpec((None, tq, hd), lambda bi, h, qi: (bi, qi, h)),
        out_shape=jax.ShapeDtypeStruct((b, t, d), BF16),
        scratch_shapes=[pltpu.VMEM((t, hd), BF16)],
        compiler_params=_params(("parallel", "parallel", "arbitrary"),
                                5 * t * hd * 2 + 4 * tq * hd * 2 + 2 * tq * (left + tq) * 4
                                + 8 * tq * (left + tq) * 4 + t * hd * 8),
        name="ca_attention",
    )(qkv, qkv, qkv, q_gain.reshape(1, hd).astype(F32), k_gain.reshape(1, hd).astype(F32), bias)


def _seg_ones(n, seg):
    r = lax.broadcasted_iota(jnp.int32, (n, n), 0) // seg
    c = lax.broadcasted_iota(jnp.int32, (n, n), 1) // seg
    return (r == c).astype(BF16)


def _segsum(x, ones):
    hi, mid, lo = _split3(x)
    return _dot(hi, ones) + _dot(mid, ones) + _dot(lo, ones)


def _rwkv_pre_kernel(k_ref, a_ref, kk_ref, ka_ref, kmod_ref, aa_ref, bb_ref):
    k = k_ref[...]
    al = a_ref[...]
    ones = _seg_ones(k.shape[1], RWKV_HEAD_DIM)
    kk = k * kk_ref[...]
    kk = kk * lax.rsqrt(jnp.maximum(_segsum(kk * kk, ones), 1e-24))
    kmod_ref[...] = k * (1.0 + (al - 1.0) * ka_ref[...])
    aa_ref[...] = -kk
    bb_ref[...] = kk * al


def rwkv_pre(k, al, k_k, k_a):
    n, d = k.shape
    gw = _tile(d, RWKV_G * RWKV_HEAD_DIM)
    tr = _tile(n, 2 * ROW_TILE, 8)
    blk = pl.BlockSpec((tr, gw), lambda i, j: (i, j))
    vec = pl.BlockSpec((1, gw), lambda i, j: (0, j))
    shp = jax.ShapeDtypeStruct((n, d), F32)
    return pl.pallas_call(
        _rwkv_pre_kernel,
        grid=(n // tr, d // gw),
        in_specs=[blk, blk, vec, vec],
        out_specs=[blk, blk, blk],
        out_shape=[shp, shp, shp],
        compiler_params=_params(("parallel", "parallel"), 20 * tr * gw * 4),
        name="rwkv_pre",
    )(k, al, k_k.reshape(1, d).astype(F32), k_a.reshape(1, d).astype(F32))


def _rwkv_scan_kernel(r_ref, lw_ref, k_ref, v_ref, aa_ref, bb_ref, y_ref, mt_ref, *, chunk):
    tb, gw = r_ref.shape
    L = chunk
    G = gw // L
    ti = pl.program_id(2)

    @pl.when(ti == 0)
    def _():
        mt_ref[...] = jnp.zeros_like(mt_ref)

    bd = (lax.broadcasted_iota(jnp.int32, (gw, gw), 0) // L) == (lax.broadcasted_iota(jnp.int32, (gw, gw), 1) // L)
    row = lax.broadcasted_iota(jnp.int32, (L, gw), 0)
    col = lax.broadcasted_iota(jnp.int32, (L, gw), 1) % L
    tri_incl = col <= row
    tri_strict = col < row
    eye = (col == row).astype(F32)
    ltri = (lax.broadcasted_iota(jnp.int32, (L, L), 1) <= lax.broadcasted_iota(jnp.int32, (L, L), 0)).astype(BF16)
    zero_b = jnp.zeros((), BF16)

    def stack(xb):
        return jnp.where(bd, jnp.concatenate([xb] * G, axis=0), zero_b)

    def dot3(dot, xs, ys):
        return dot(xs[0], ys[0]) + dot(xs[0], ys[1]) + dot(xs[1], ys[0])

    def prod(x, y):
        yh, ym = _split2(y)
        return dot3(_dot, _split2(x), (stack(yh), stack(ym)))

    def prod_nt(xs, y):
        yh, ym = _split2(y)
        return dot3(_dot_nt, xs, (stack(yh), stack(ym)))

    def body(c, mt):
        sl = pl.ds(pl.multiple_of(c * L, L), L)
        r, lw, k, v, aa, bb = (ref[sl, :] for ref in (r_ref, lw_ref, k_ref, v_ref, aa_ref, bb_ref))
        cw = functools.reduce(lambda s, part: s + _dot(ltri, part), _split3(lw), jnp.zeros((L, gw), F32))
        cmid = cw[L // 2 - 1:L // 2, :]
        ctot = cw[L - 1:L, :]
        at_state = aa * jnp.exp(cw - lw)
        rt_state = r * jnp.exp(cw)
        to_mid = jnp.exp(-cmid)
        e_mid = jnp.exp(cmid - cw)
        e_tot = jnp.exp(ctot - cw)
        ats = _split2(at_state * to_mid)
        rts = _split2(rt_state * to_mid)
        bt = bb * e_mid
        kt = k * e_mid
        a_ab = jnp.where(tri_strict, prod_nt(ats, bt), 0.0)
        a_ak = jnp.where(tri_strict, prod_nt(ats, kt), 0.0)
        a_rb = jnp.where(tri_incl, prod_nt(rts, bt), 0.0)
        a_rk = jnp.where(tri_incl, prod_nt(rts, kt), 0.0)
        tmat = eye + a_ab
        pw = a_ab
        for _ in range(L.bit_length() - 2):
            pw = prod(pw, pw)
            tmat = tmat + prod(pw, tmat)
        mts = _split2(mt)
        x = dot3(_dot_nt, _split2(at_state), mts)
        u = prod(tmat, x + prod(a_ak, v))
        y = dot3(_dot_nt, _split2(rt_state), mts) + prod(a_rb, u) + prod(a_rk, v)
        y_ref[sl, :] = y
        upd = dot3(_dot_tn, _split2(jnp.concatenate([u, v], axis=0)),
                   _split2(jnp.concatenate([bb * e_tot, k * e_tot], axis=0)))
        return mt * jnp.exp(ctot) + jnp.where(bd, upd, 0.0)

    mt_ref[...] = lax.fori_loop(0, tb // L, body, mt_ref[...])


def rwkv_scan(r, lw, k, v, aa, bb, seq):
    n, d = r.shape
    b = n // seq
    L = RWKV_L
    assert L == RWKV_HEAD_DIM and L & (L - 1) == 0
    gw = _tile(d, RWKV_G * RWKV_HEAD_DIM)
    tb = _tile(seq, RWKV_TB, L)
    blk = pl.BlockSpec((None, tb, gw), lambda bi, gi, ti: (bi, ti, gi))
    args = [a.reshape(b, seq, d) for a in (r, lw, k, v, aa, bb)]
    y = pl.pallas_call(
        functools.partial(_rwkv_scan_kernel, chunk=L),
        grid=(b, d // gw, seq // tb),
        in_specs=[blk] * 6,
        out_specs=blk,
        out_shape=jax.ShapeDtypeStruct((b, seq, d), F32),
        scratch_shapes=[pltpu.VMEM((gw, gw), F32)],
        compiler_params=_params(("parallel", "parallel", "arbitrary"), 14 * tb * gw * 4 + 40 * gw * gw * 4),
        name="rwkv_scan",
    )(*args)
    return y.reshape(n, d)


def _rwkv_post_kernel(y_ref, r_ref, k_ref, v_ref, g_ref, rk_ref, lnw_ref, lnb_ref, o_ref):
    gw = y_ref.shape[1]
    ones = _seg_ones(gw, RWKV_HEAD_DIM)
    inv = 1.0 / RWKV_HEAD_DIM
    y = y_ref[...]
    yc = y - _segsum(y, ones) * inv
    var = _segsum(yc * yc, ones) * inv
    yn = yc * lax.rsqrt(var + RWKV_GN_EPS) * lnw_ref[...] + lnb_ref[...]
    bonus = _segsum(r_ref[...] * k_ref[...] * rk_ref[...], ones) * v_ref[...]
    o_ref[...] = ((yn + bonus) * g_ref[...]).astype(o_ref.dtype)


def rwkv_post(y, r, k, v, g, r_k, ln_w, ln_b):
    n, d = y.shape
    gw = _tile(d, RWKV_G * RWKV_HEAD_DIM)
    tr = _tile(n, 2 * ROW_TILE, 8)
    blk = pl.BlockSpec((tr, gw), lambda i, j: (i, j))
    vec = pl.BlockSpec((1, gw), lambda i, j: (0, j))
    return pl.pallas_call(
        _rwkv_post_kernel,
        grid=(n // tr, d // gw),
        in_specs=[blk] * 5 + [vec] * 3,
        out_specs=blk,
        out_shape=jax.ShapeDtypeStruct((n, d), BF16),
        compiler_params=_params(("parallel", "parallel"), 30 * tr * gw * 4),
        name="rwkv_post",
    )(y, r, k, v, g, *[p.reshape(1, d).astype(F32) for p in (r_k, ln_w, ln_b)])


def _router_kernel(h_ref, g_ref, w_ref, o_ref, *, n_exp):
    xh, xm = _split2(_rms(h_ref[...], g_ref[...]))
    wh, wm = w_ref[0], w_ref[1]
    logits = _dot(xh, wh) + _dot(xh, wm) + _dot(xm, wh)
    lane = lax.broadcasted_iota(jnp.int32, logits.shape, 1)
    valid = lane < n_exp
    logits = jnp.where(valid, logits, NEG_BIG)
    e = jnp.exp(logits - jnp.max(logits, axis=1, keepdims=True))
    probs = jnp.where(valid, e / jnp.sum(e, axis=1, keepdims=True), -1.0)
    big = logits.shape[1]
    p1 = jnp.max(probs, axis=1, keepdims=True)
    i1 = jnp.min(jnp.where(probs == p1, lane, big), axis=1, keepdims=True)
    rest = jnp.where(lane == i1, -1.0, probs)
    p2 = jnp.max(rest, axis=1, keepdims=True)
    i2 = jnp.min(jnp.where(rest == p2, lane, big), axis=1, keepdims=True)
    den = p1 + p2
    o_ref[...] = jnp.where(lane == i1, p1 / den, 0.0) + jnp.where(lane == i2, p2 / den, 0.0)


def moe_gates(h, gain, router):
    n, d = h.shape
    n_exp = router.shape[1]
    assert n_exp <= V7X_LANES and TOP_K == 2
    w = jnp.pad(router.astype(F32), ((0, 0), (0, V7X_LANES - n_exp)))
    wh = w.astype(BF16)
    w2 = jnp.stack([wh, (w - wh.astype(F32)).astype(BF16)])
    tr = _tile(n, ROW_TILE, 8)
    return pl.pallas_call(
        functools.partial(_router_kernel, n_exp=n_exp),
        grid=(n // tr,),
        in_specs=[pl.BlockSpec((tr, d), lambda i: (i, 0)),
                  pl.BlockSpec((1, d), lambda i: (0, 0)),
                  pl.BlockSpec((2, d, V7X_LANES), lambda i: (0, 0, 0))],
        out_specs=pl.BlockSpec((tr, V7X_LANES), lambda i: (i, 0)),
        out_shape=jax.ShapeDtypeStruct((n, V7X_LANES), F32),
        compiler_params=_params(("parallel",), 2 * tr * d * 4 + 4 * tr * d * 4 + 4 * d * V7X_LANES * 2),
        name="moe_gates",
    )(h, gain.reshape(1, d), w2)


def _epi_tanh(acc):
    return jnp.tanh(acc)


def _epi_sigmoid(acc):
    return 1.0 / (1.0 + jnp.exp(-acc))


def _epi_bias_sigmoid(acc, bias):
    return 1.0 / (1.0 + jnp.exp(-(acc + bias)))


def _epi_log_decay(acc, w0):
    x = -(acc + w0)
    softplus = jnp.maximum(x, 0.0) + jnp.log1p(jnp.exp(-jnp.abs(x)))
    return -jnp.exp(-softplus - 0.5)


def _bf(w):
    return w.astype(BF16)


def sb_layer(h, gain, w_qkv, w_o, batch):
    n, d = h.shape
    xn = rmsnorm(h, gain)
    qkv = matmul(xn, _bf(w_qkv), BF16, name="sb_qkv")
    o = sb_attention(qkv.reshape(batch, n // batch, 3 * d), SB_HEADS)
    return matmul(o.reshape(n, d), _bf(w_o), F32, epi=_epi_residual, fulls=(h,), tn=MM_TN // 2, name="sb_out")


def ca_layer(h, gain, w_qkv, q_norm, k_norm, rel_bias, w_o, batch):
    n, d = h.shape
    xn = rmsnorm(h, gain)
    qkv = matmul(xn, _bf(w_qkv), BF16, name="ca_qkv")
    o = ca_attention(qkv.reshape(batch, n // batch, 3 * d), q_norm, k_norm, rel_bias, CA_HEADS)
    return matmul(o.reshape(n, d), _bf(w_o), F32, epi=_epi_residual, fulls=(h,), tn=MM_TN // 2, name="ca_out")


def rwkv_layer(h, gain, mu, w_rkv, w0, w1, w2, a0, a1, a2, g1, g2, k_k, k_a, r_k, ln_w, ln_b, w_o, batch):
    n, d = h.shape
    seq = n // batch
    m_r, m_w, m_k, m_v, m_a, m_g = rmsnorm_shift(h, gain, mu, seq)
    r = matmul(m_r, _bf(w_rkv[0]), F32, name="rwkv_r")
    k = matmul(m_k, _bf(w_rkv[1]), F32, name="rwkv_k")
    v = matmul(m_v, _bf(w_rkv[2]), F32, name="rwkv_v")
    lw = matmul(matmul(m_w, _bf(w1), BF16, epi=_epi_tanh, name="rwkv_w1"), _bf(w2), F32,
                epi=_epi_log_decay, cols=(w0,), name="rwkv_w2")
    al = matmul(matmul(m_a, _bf(a1), BF16, name="rwkv_a1"), _bf(a2), F32,
                epi=_epi_bias_sigmoid, cols=(a0,), name="rwkv_a2")
    g = matmul(matmul(m_g, _bf(g1), BF16, epi=_epi_sigmoid, name="rwkv_g1"), _bf(g2), F32, name="rwkv_g2")
    kmod, aa, bb = rwkv_pre(k, al, k_k, k_a)
    y = rwkv_scan(r, lw, kmod, v, aa, bb, seq)
    out = rwkv_post(y, r, kmod, v, g, r_k, ln_w, ln_b)
    return matmul(out, _bf(w_o), F32, epi=_epi_residual, fulls=(h,), tn=MM_TN // 2, name="rwkv_out")


def dense_ffn(h, gain, w_gate_up, w_down):
    xn = rmsnorm(h, gain)
    mid = glu_matmul(xn, _bf(w_gate_up)[None], name="ffn_up")
    return matmul(mid, _bf(w_down), F32, epi=_epi_residual, fulls=(h,), tn=MM_TN // 2, name="ffn_down")


def moe_ffn(h, gain, router, w_gate_up, w_down):
    n_exp, f, d = w_down.shape
    xn = rmsnorm(h, gain)
    gates = moe_gates(h, gain, router)
    mid = glu_matmul(xn, _bf(w_gate_up), gates, name="moe_up")
    return matmul(mid, _bf(w_down).reshape(n_exp * f, d), F32, epi=_epi_residual, fulls=(h,),
                  tn=MM_TN // 2, name="moe_down")


def kernel(x, mix_norm, ffn_norm, sb_w_qkv, sb_w_o, rwkv_mu, rwkv_w_rkv, rwkv_w0, rwkv_w1, rwkv_w2, rwkv_a0, rwkv_a1, rwkv_a2, rwkv_g1, rwkv_g2, rwkv_k_k, rwkv_k_a, rwkv_r_k, rwkv_ln_w, rwkv_ln_b, rwkv_w_o, ca_w_qkv, ca_q_norm, ca_k_norm, ca_rel_bias, ca_w_o, ffn_w_gate_up, ffn_w_down, moe_router, moe_w_gate_up, moe_w_down):
    batch, seq, d = x.shape
    h = x.reshape(batch * seq, d)
    for i in range(mix_norm.shape[0]):
        kind, j = i % N_MIXERS, i // N_MIXERS
        if kind == 0:
            h = sb_layer(h, mix_norm[i], sb_w_qkv[j], sb_w_o[j], batch)
        elif kind == 1:
            h = rwkv_layer(h, mix_norm[i], rwkv_mu[j], rwkv_w_rkv[j], rwkv_w0[j], rwkv_w1[j], rwkv_w2[j],
                           rwkv_a0[j], rwkv_a1[j], rwkv_a2[j], rwkv_g1[j], rwkv_g2[j], rwkv_k_k[j],
                           rwkv_k_a[j], rwkv_r_k[j].reshape(-1), rwkv_ln_w[j], rwkv_ln_b[j], rwkv_w_o[j], batch)
        else:
            h = ca_layer(h, mix_norm[i], ca_w_qkv[j], ca_q_norm[j], ca_k_norm[j], ca_rel_bias[j], ca_w_o[j], batch)
        f = i // 2
        if i % 2 == 0:
            h = dense_ffn(h, ffn_norm[i], ffn_w_gate_up[f], ffn_w_down[f])
        else:
            h = moe_ffn(h, ffn_norm[i], moe_router[f], moe_w_gate_up[f], moe_w_down[f])
    return h.reshape(batch, seq, d)
```

```python
import functools

import jax
import jax.numpy as jnp
from jax import lax
from jax.experimental import pallas as pl
from jax.experimental.pallas import tpu as pltpu

F32 = jnp.float32
BF16 = jnp.bfloat16

NORM_EPS = 1e-6
SB_HEADS = 32
RWKV_HEAD_DIM = 64
RWKV_GN_EPS = 64e-5
CA_HEADS = 32
CHUNK = 64
LEFT_CHUNKS = 8
REL_CLIP = 256
TOP_K = 2
N_MIXERS = 3

V7X_LANES = 128
V7X_MXU_DIM = 256
V7X_VMEM_BYTES = 64 * 1024 * 1024
V7X_VMEM_CAP = 56 * 1024 * 1024

MM_TM = 1024
MM_TN = 1024
MM_TK = 2048
ROW_TILE = 256
SB_TILE = 256
SB_HEADS_PER_STEP = 4
CA_TILE = 256
RWKV_L = 64
RWKV_G = 4
RWKV_TB = 256
RWKV_STREAMS = 4
SCAN_SPLIT_MAIN = True
SCAN_SPLIT_INV = False


def _tile(dim, pref, align=V7X_LANES):
    if dim <= pref:
        return dim
    t = (pref // align) * align
    while t >= align:
        if dim % t == 0:
            return t
        t -= align
    return dim


def _params(sem, vmem_bytes):
    limit = int(min(max(vmem_bytes * 5 // 4 + (4 << 20), 16 << 20), V7X_VMEM_CAP))
    return pltpu.CompilerParams(dimension_semantics=sem, vmem_limit_bytes=limit)


def _rms(x, gain):
    return x * lax.rsqrt(jnp.mean(x * x, axis=-1, keepdims=True) + NORM_EPS) * gain


def _rmsnorm_kernel(h_ref, g_ref, o_ref):
    o_ref[...] = _rms(h_ref[...], g_ref[...]).astype(o_ref.dtype)


def rmsnorm(h, gain, out_dtype=BF16):
    n, d = h.shape
    tr = _tile(n, ROW_TILE, 8)
    return pl.pallas_call(
        _rmsnorm_kernel,
        grid=(n // tr,),
        in_specs=[pl.BlockSpec((tr, d), lambda i: (i, 0)),
                  pl.BlockSpec((1, d), lambda i: (0, 0))],
        out_specs=pl.BlockSpec((tr, d), lambda i: (i, 0)),
        out_shape=jax.ShapeDtypeStruct((n, d), out_dtype),
        compiler_params=_params(("parallel",), 2 * tr * d * 6),
        name="rmsnorm",
    )(h, gain.reshape(1, d))


def _rmsnorm_shift_kernel(h_ref, hp_ref, g_ref, mu_ref, *o_refs, tiles_per_seq):
    i = pl.program_id(0)
    g = g_ref[...]
    xn = _rms(h_ref[...], g)
    prev_last = _rms(hp_ref[...], g)[7:8, :]
    prev_last = jnp.where(i % tiles_per_seq == 0, 0.0, prev_last)
    rows = lax.broadcasted_iota(jnp.int32, xn.shape, 0)
    shifted = jnp.where(rows == 0, prev_last, pltpu.roll(xn, 1, axis=0))
    dx = shifted - xn
    for m, o_ref in enumerate(o_refs):
        o_ref[...] = (xn + dx * mu_ref[m:m + 1, :]).astype(o_ref.dtype)


def rmsnorm_shift(h, gain, mu, seq):
    n, d = h.shape
    nm = mu.shape[0]
    tr = _tile(seq, ROW_TILE, 8)
    tps = seq // tr
    sub = tr // 8
    return pl.pallas_call(
        functools.partial(_rmsnorm_shift_kernel, tiles_per_seq=tps),
        grid=(n // tr,),
        in_specs=[pl.BlockSpec((tr, d), lambda i: (i, 0)),
                  pl.BlockSpec((8, d), lambda i: (jnp.maximum(i * sub - 1, 0), 0)),
                  pl.BlockSpec((1, d), lambda i: (0, 0)),
                  pl.BlockSpec((nm, d), lambda i: (0, 0))],
        out_specs=[pl.BlockSpec((tr, d), lambda i: (i, 0))] * nm,
        out_shape=[jax.ShapeDtypeStruct((n, d), BF16)] * nm,
        compiler_params=_params(("parallel",), 2 * tr * d * (4 + 2 * nm) + 4 * tr * d * 4),
        name="rmsnorm_shift",
    )(h, h, gain.reshape(1, d), mu)


def _mm_kernel(*refs, n_extra, epi, nk):
    a_ref, b_ref = refs[0], refs[1]
    extra = refs[2:2 + n_extra]
    o_ref = refs[2 + n_extra]

    def finish(acc):
        o_ref[...] = epi(acc, *[e[...] for e in extra]).astype(o_ref.dtype)

    part = jnp.dot(a_ref[...], b_ref[...], preferred_element_type=F32)
    if nk == 1:
        finish(part)
        return
    acc_ref = refs[3 + n_extra]
    k = pl.program_id(2)

    @pl.when(k == 0)
    def _():
        acc_ref[...] = part

    @pl.when(jnp.logical_and(k > 0, k < nk - 1))
    def _():
        acc_ref[...] += part

    @pl.when(k == nk - 1)
    def _():
        finish(acc_ref[...] + part)


def _epi_id(acc):
    return acc


def matmul(a, b, out_dtype, epi=_epi_id, cols=(), fulls=(), tm=MM_TM, tn=MM_TN, tk=MM_TK, name="matmul"):
    m, kd = a.shape
    _, n = b.shape
    tm, tn = _tile(m, tm, 8), _tile(n, tn)
    tk = kd if kd <= 2 * tk else _tile(kd, tk)
    nk = kd // tk
    extras = [c.reshape(1, n).astype(F32) for c in cols] + list(fulls)
    in_specs = [pl.BlockSpec((tm, tk), lambda i, j, k: (i, k)),
                pl.BlockSpec((tk, tn), lambda i, j, k: (k, j))]
    in_specs += [pl.BlockSpec((1, tn), lambda i, j, k: (0, j)) for _ in cols]
    in_specs += [pl.BlockSpec((tm, tn), lambda i, j, k: (i, j)) for _ in fulls]
    osize = jnp.dtype(out_dtype).itemsize
    vmem = 2 * (tm * tk + tk * tn) * a.dtype.itemsize + 2 * tm * tn * osize
    vmem += sum(2 * tm * tn * f.dtype.itemsize for f in fulls) + (tm * tn * 4 if nk > 1 else 0)
    vmem += tm * tn * 4
    return pl.pallas_call(
        functools.partial(_mm_kernel, n_extra=len(extras), epi=epi, nk=nk),
        grid=(m // tm, n // tn, nk),
        in_specs=in_specs,
        out_specs=pl.BlockSpec((tm, tn), lambda i, j, k: (i, j)),
        out_shape=jax.ShapeDtypeStruct((m, n), out_dtype),
        scratch_shapes=[pltpu.VMEM((tm, tn), F32)] if nk > 1 else [],
        compiler_params=_params(("parallel", "parallel", "arbitrary"), vmem),
        name=name,
    )(a, b, *extras)


def _epi_residual(acc, res):
    return res + acc


def _silu(x):
    return x * (1.0 / (1.0 + jnp.exp(-x)))


def _glu_kernel(*refs, nk, n_exp, gated):
    if gated:
        a_ref, bg_ref, bu_ref, gate_ref, o_ref = refs[:5]
        scratch = refs[5:]
    else:
        a_ref, bg_ref, bu_ref, o_ref = refs[:4]
        gate_ref = None
        scratch = refs[4:]

    def finish(g, u):
        y = _silu(g) * u
        if gated:
            e = pl.program_id(1)
            gates = gate_ref[...]
            lane = lax.broadcasted_iota(jnp.int32, gates.shape, 1)
            y = y * jnp.sum(jnp.where(lane == e, gates, 0.0), axis=1, keepdims=True)
        o_ref[...] = y.astype(o_ref.dtype)

    a = a_ref[...]
    pg = jnp.dot(a, bg_ref[...], preferred_element_type=F32)
    pu = jnp.dot(a, bu_ref[...], preferred_element_type=F32)
    if nk == 1:
        finish(pg, pu)
        return
    accg_ref, accu_ref = scratch
    k = pl.program_id(3)

    @pl.when(k == 0)
    def _():
        accg_ref[...] = pg
        accu_ref[...] = pu

    @pl.when(jnp.logical_and(k > 0, k < nk - 1))
    def _():
        accg_ref[...] += pg
        accu_ref[...] += pu

    @pl.when(k == nk - 1)
    def _():
        finish(accg_ref[...] + pg, accu_ref[...] + pu)


def glu_matmul(a, w, gates=None, tm=MM_TM, tn=MM_TN // 2, tk=MM_TK, name="glu_matmul"):
    m, kd = a.shape
    n_exp, _, f2 = w.shape
    f = f2 // 2
    tm = _tile(m, tm, 8)
    tn = _tile(f, tn)
    tk = kd if kd <= 2 * tk else _tile(kd, tk)
    nk = kd // tk
    nj = f // tn
    gated = gates is not None
    in_specs = [pl.BlockSpec((tm, tk), lambda i, e, j, k: (i, k)),
                pl.BlockSpec((None, tk, tn), lambda i, e, j, k: (e, k, j)),
                pl.BlockSpec((None, tk, tn), lambda i, e, j, k: (e, k, nj + j))]
    args = [a, w, w]
    if gated:
        in_specs.append(pl.BlockSpec((tm, gates.shape[1]), lambda i, e, j, k: (i, 0)))
        args.append(gates)
    vmem = 2 * (tm * tk + 2 * tk * tn) * 2 + 2 * tm * tn * 2 + 4 * tm * tn * 4 + 2 * tm * 128 * 4
    return pl.pallas_call(
        functools.partial(_glu_kernel, nk=nk, n_exp=n_exp, gated=gated),
        grid=(m // tm, n_exp, nj, nk),
        in_specs=in_specs,
        out_specs=pl.BlockSpec((tm, tn), lambda i, e, j, k: (i, e * nj + j)),
        out_shape=jax.ShapeDtypeStruct((m, n_exp * f), BF16),
        scratch_shapes=[pltpu.VMEM((tm, tn), F32), pltpu.VMEM((tm, tn), F32)] if nk > 1 else [],
        compiler_params=_params(("parallel", "parallel", "parallel", "arbitrary"), vmem),
        name=name,
    )(*args)


def _split2(x):
    hi = x.astype(BF16)
    return hi, (x - hi.astype(F32)).astype(BF16)


def _split3(x):
    hi = x.astype(BF16)
    r1 = x - hi.astype(F32)
    mid = r1.astype(BF16)
    return hi, mid, (r1 - mid.astype(F32)).astype(BF16)


def _dot(a, b):
    return jnp.dot(a, b, preferred_element_type=F32)


def _dot_nt(a, b):
    return lax.dot_general(a, b, (((1,), (1,)), ((), ())), preferred_element_type=F32)


def _dot_tn(a, b):
    return lax.dot_general(a, b, (((0,), (0,)), ((), ())), preferred_element_type=F32)


def _sb_kernel(q_ref, k_ref, v_ref, o_ref, *, scale, hd):
    t = q_ref.shape[0]
    n_h = q_ref.shape[1] // hd
    qi = pl.program_id(2)
    row = lax.broadcasted_iota(jnp.int32, (t, t), 0)
    col = lax.broadcasted_iota(jnp.int32, (t, t), 1)
    later = (row > col).astype(BF16)
    past = col < row

    def block(kb, carry, diag):
        start = pl.multiple_of(kb * t, t)
        heads = range(n_h)
        lanes = [slice(h * hd, (h + 1) * hd) for h in heads]
        z = [_dot_nt(q_ref[:, lanes[h]], k_ref[pl.ds(start, t), lanes[h]]) * scale for h in heads]
        ls, lk, parts = [], [], []
        for h in heads:
            ls.append(jnp.minimum(z[h], 0.0) - jnp.log(1.0 + jnp.exp(-jnp.abs(z[h]))))
            lk_h = ls[h] - z[h]
            lk.append(jnp.where(past, lk_h, 0.0) if diag else lk_h)
            parts.append(_split2(lk[h]))
        suf = [_dot(parts[h][0], later) + _dot(parts[h][1], later) for h in heads]
        att = []
        for h in heads:
            a = jnp.exp(ls[h] + suf[h] + carry[h][1])
            att.append((jnp.where(past, a, 0.0) if diag else a).astype(BF16))
        return tuple((carry[h][0] + _dot(att[h], v_ref[pl.ds(start, t), lanes[h]]),
                      carry[h][1] + jnp.sum(lk[h], axis=1, keepdims=True)) for h in heads)

    init = tuple((jnp.zeros((t, hd), F32), jnp.zeros((t, 1), F32)) for _ in range(n_h))
    carry = lax.fori_loop(0, qi, lambda it, cr: block(qi - 1 - it, cr, False), block(qi, init, True))
    for h in range(n_h):
        o_ref[:, h * hd:(h + 1) * hd] = carry[h][0].astype(o_ref.dtype)


def sb_attention(qkv, n_heads):
    b, t, d3 = qkv.shape
    d = d3 // 3
    hd = d // n_heads
    tq = _tile(t, SB_TILE, 8)
    hps = SB_HEADS_PER_STEP if n_heads % SB_HEADS_PER_STEP == 0 else 1
    w = hps * hd
    ng = n_heads // hps
    return pl.pallas_call(
        functools.partial(_sb_kernel, scale=hd ** -0.5, hd=hd),
        grid=(b, ng, t // tq),
        in_specs=[pl.BlockSpec((None, tq, w), lambda bi, g, qi: (bi, qi, g)),
                  pl.BlockSpec((None, t, w), lambda bi, g, qi: (bi, 0, ng + g)),
                  pl.BlockSpec((None, t, w), lambda bi, g, qi: (bi, 0, 2 * ng + g))],
        out_specs=pl.BlockSpec((None, tq, w), lambda bi, g, qi: (bi, qi, g)),
        out_shape=jax.ShapeDtypeStruct((b, t, d), BF16),
        compiler_params=_params(("parallel", "parallel", "arbitrary"),
                                4 * t * w * 2 + 4 * tq * w * 2 + 24 * hps * tq * tq * 4),
        name="sb_attention",
    )(qkv, qkv, qkv)


NEG_BIG = -1e30


def _ca_kernel(q_ref, k_ref, v_ref, qg_ref, kg_ref, bias_ref, o_ref, kn_ref, *, scale, n_win):
    t = q_ref.shape[0]
    qi = pl.program_id(2)

    @pl.when(qi == 0)
    def _():
        kn_ref[...] = _rms(k_ref[...].astype(F32), kg_ref[...]).astype(BF16)

    qn = _rms(q_ref[...].astype(F32), qg_ref[...]).astype(BF16)
    scores, starts = [], []
    for w in range(n_win):
        kb = qi - (n_win - 1) + w
        start = pl.multiple_of(jnp.maximum(kb, 0) * t, t)
        s = _dot_nt(qn, kn_ref[pl.ds(start, t), :]) * scale + bias_ref[:, w * t:(w + 1) * t]
        scores.append(jnp.where(kb >= 0, s, NEG_BIG))
        starts.append(start)
    m = functools.reduce(jnp.maximum, [jnp.max(s, axis=1, keepdims=True) for s in scores])
    acc = jnp.zeros(o_ref.shape, F32)
    den = jnp.zeros((t, 1), F32)
    for s, start in zip(scores, starts):
        p = jnp.exp(s - m)
        den = den + jnp.sum(p, axis=1, keepdims=True)
        acc = acc + _dot(p.astype(BF16), v_ref[pl.ds(start, t), :])
    o_ref[...] = (acc / den).astype(o_ref.dtype)


def _ca_bias_table(rel_bias, t):
    left = LEFT_CHUNKS * CHUNK
    i = jnp.arange(t)[:, None]
    j = jnp.arange(left + t)[None, :]
    rel = jnp.clip(i + left - j, -REL_CLIP, REL_CLIP) + REL_CLIP
    qc = (i + left) // CHUNK
    kc = j // CHUNK
    allowed = jnp.logical_and(kc <= qc, kc >= qc - LEFT_CHUNKS)
    return jnp.where(allowed[None], rel_bias.astype(F32)[:, rel], NEG_BIG)


def ca_attention(qkv, q_gain, k_gain, rel_bias, n_heads):
    b, t, d3 = qkv.shape
    d = d3 // 3
    hd = d // n_heads
    tq = _tile(t, CA_TILE, CHUNK)
    left = LEFT_CHUNKS * CHUNK
    assert left % tq == 0 and tq % CHUNK == 0
    n_win = left // tq + 1
    bias = _ca_bias_table(rel_bias, tq)
    return pl.pallas_call(
        functools.partial(_ca_kernel, scale=hd ** -0.5, n_win=n_win),
        grid=(b, n_heads, t // tq),
        in_specs=[pl.BlockSpec((None, tq, hd), lambda bi, h, qi: (bi, qi, h)),
                  pl.BlockSpec((None, t, hd), lambda bi, h, qi: (bi, 0, n_heads + h)),
                  pl.BlockSpec((None, t, hd), lambda bi, h, qi: (bi, 0, 2 * n_heads + h)),
                  pl.BlockSpec((1, hd), lambda bi, h, qi: (0, 0)),
                  pl.BlockSpec((1, hd), lambda bi, h, qi: (0, 0)),
                  pl.BlockSpec((None, tq, left + tq), lambda bi, h, qi: (h, 0, 0))],
        out_specs=pl.BlockSpec((None, tq, hd), lambda bi, h, qi: (bi, qi, h)),
        out_shape=jax.ShapeDtypeStruct((b, t, d), BF16),
        scratch_shapes=[pltpu.VMEM((t, hd), BF16)],
        compiler_params=_params(("parallel", "parallel", "arbitrary"),
                                5 * t * hd * 2 + 4 * tq * hd * 2 + 2 * tq * (left + tq) * 4
                                + 8 * tq * (left + tq) * 4 + t * hd * 8),
        name="ca_attention",
    )(qkv, qkv, qkv, q_gain.reshape(1, hd).astype(F32), k_gain.reshape(1, hd).astype(F32), bias)


def _seg_ones(n, seg):
    r = lax.broadcasted_iota(jnp.int32, (n, n), 0) // seg
    c = lax.broadcasted_iota(jnp.int32, (n, n), 1) // seg
    return (r == c).astype(BF16)


def _segsum(x, ones):
    hi, mid, lo = _split3(x)
    return _dot(hi, ones) + _dot(mid, ones) + _dot(lo, ones)


def _rwkv_pre_kernel(k_ref, a_ref, kk_ref, ka_ref, kmod_ref, aa_ref, bb_ref):
    k = k_ref[...]
    al = a_ref[...]
    ones = _seg_ones(k.shape[1], RWKV_HEAD_DIM)
    kk = k * kk_ref[...]
    kk = kk * lax.rsqrt(jnp.maximum(_segsum(kk * kk, ones), 1e-24))
    kmod_ref[...] = k * (1.0 + (al - 1.0) * ka_ref[...])
    aa_ref[...] = -kk
    bb_ref[...] = kk * al


def rwkv_pre(k, al, k_k, k_a):
    n, d = k.shape
    gw = _tile(d, RWKV_G * RWKV_HEAD_DIM)
    tr = _tile(n, 2 * ROW_TILE, 8)
    blk = pl.BlockSpec((tr, gw), lambda i, j: (i, j))
    vec = pl.BlockSpec((1, gw), lambda i, j: (0, j))
    shp = jax.ShapeDtypeStruct((n, d), F32)
    return pl.pallas_call(
        _rwkv_pre_kernel,
        grid=(n // tr, d // gw),
        in_specs=[blk, blk, vec, vec],
        out_specs=[blk, blk, blk],
        out_shape=[shp, shp, shp],
        compiler_params=_params(("parallel", "parallel"), 20 * tr * gw * 4),
        name="rwkv_pre",
    )(k, al, k_k.reshape(1, d).astype(F32), k_a.reshape(1, d).astype(F32))


def _rwkv_scan_kernel(r_ref, lw_ref, k_ref, v_ref, aa_ref, bb_ref, y_ref, mt_ref, *, chunk, gw):
    tb = r_ref.shape[0]
    L = chunk
    G = gw // L
    streams = range(r_ref.shape[1] // gw)
    chunks = range(tb // L)
    ti = pl.program_id(2)

    @pl.when(ti == 0)
    def _():
        mt_ref[...] = jnp.zeros_like(mt_ref)

    bd = (lax.broadcasted_iota(jnp.int32, (gw, gw), 0) // L) == (lax.broadcasted_iota(jnp.int32, (gw, gw), 1) // L)
    row = lax.broadcasted_iota(jnp.int32, (L, gw), 0)
    col = lax.broadcasted_iota(jnp.int32, (L, gw), 1) % L
    tri_incl = col <= row
    tri_strict = col < row
    eye = (col == row).astype(F32)
    ltri = (lax.broadcasted_iota(jnp.int32, (L, L), 1) <= lax.broadcasted_iota(jnp.int32, (L, L), 0)).astype(BF16)
    zero_b = jnp.zeros((), BF16)

    def stack(xb):
        return jnp.where(bd, jnp.concatenate([xb] * G, axis=0), zero_b)

    def parts(y, two):
        return _split2(y) if two else (y.astype(BF16),)

    def mm(dot, x, ws):
        xh, xm = _split2(x)
        out = dot(xh, ws[0]) + dot(xm, ws[0])
        return out + dot(xh, ws[1]) if len(ws) > 1 else out

    def prod(x, y, two):
        return mm(_dot, x, [stack(p) for p in parts(y, two)])

    def prod_nt(x, y, two):
        return mm(_dot_nt, x, [stack(p) for p in parts(y, two)])

    def rows(*xs):
        return jnp.concatenate(xs, axis=0)

    units = [(c, s) for c in chunks for s in streams]
    loc = {}
    for c, s in units:
        sl = (pl.ds(c * L, L), slice(s * gw, (s + 1) * gw))
        r, lw, k, v, aa, bb = (ref[sl] for ref in (r_ref, lw_ref, k_ref, v_ref, aa_ref, bb_ref))
        cw = functools.reduce(lambda acc, part: acc + _dot(ltri, part), _split3(lw), jnp.zeros((L, gw), F32))
        cmid = cw[L // 2 - 1:L // 2, :]
        ctot = cw[L - 1:L, :]
        at_state = aa * jnp.exp(cw - lw)
        rt_state = r * jnp.exp(cw)
        to_mid = jnp.exp(-cmid)
        e_mid = jnp.exp(cmid - cw)
        e_tot = jnp.exp(ctot - cw)
        lhs = rows(at_state * to_mid, rt_state * to_mid)
        pb = prod_nt(lhs, bb * e_mid, SCAN_SPLIT_MAIN)
        pk = prod_nt(lhs, k * e_mid, SCAN_SPLIT_MAIN)
        a_ak = jnp.where(tri_strict, pk[:L], 0.0)
        a_rk = jnp.where(tri_incl, pk[L:], 0.0)
        av = prod(rows(a_ak, a_rk), v, SCAN_SPLIT_MAIN)
        loc[c, s] = dict(sl=sl, v=v, st=rows(at_state, rt_state), n=jnp.where(tri_strict, pb[:L], 0.0),
                         a_rb=jnp.where(tri_incl, pb[L:], 0.0), akv=av[:L], rkv=av[L:],
                         wk=rows(bb * e_tot, k * e_tot), decay=jnp.exp(ctot))

    n_lev = L.bit_length() - 2
    pw = {u: prod(loc[u]["n"], loc[u]["n"], SCAN_SPLIT_INV) for u in units}
    tm = {u: eye + loc[u]["n"] for u in units}
    for lev in range(n_lev):
        if lev < n_lev - 1:
            res = {u: prod(rows(tm[u], pw[u]), pw[u], SCAN_SPLIT_INV) for u in units}
            tm = {u: tm[u] + res[u][:L] for u in units}
            pw = {u: res[u][L:] for u in units}
        else:
            tm = {u: tm[u] + prod(tm[u], pw[u], SCAN_SPLIT_INV) for u in units}
    tr = {u: rows(tm[u], prod(loc[u]["a_rb"], tm[u], SCAN_SPLIT_MAIN)) for u in units}

    mt = [mt_ref[s] for s in streams]
    for c in chunks:
        xr = [mm(_dot_nt, loc[c, s]["st"], parts(mt[s], SCAN_SPLIT_MAIN)) for s in streams]
        ur = [prod(tr[c, s], xr[s][:L] + loc[c, s]["akv"], SCAN_SPLIT_MAIN) for s in streams]
        for s in streams:
            y_ref[loc[c, s]["sl"]] = xr[s][L:] + ur[s][L:] + loc[c, s]["rkv"]
        upd = [mm(_dot_tn, rows(ur[s][:L], loc[c, s]["v"]), parts(loc[c, s]["wk"], SCAN_SPLIT_MAIN)) for s in streams]
        mt = [mt[s] * loc[c, s]["decay"] + jnp.where(bd, upd[s], 0.0) for s in streams]
    for s in streams:
        mt_ref[s] = mt[s]


def rwkv_scan(r, lw, k, v, aa, bb, seq):
    n, d = r.shape
    b = n // seq
    L = RWKV_L
    assert L == RWKV_HEAD_DIM and L & (L - 1) == 0
    gw = _tile(d, RWKV_G * RWKV_HEAD_DIM)
    ns = RWKV_STREAMS if d % (RWKV_STREAMS * gw) == 0 else 1
    bw = ns * gw
    tb = _tile(seq, RWKV_TB, L)
    blk = pl.BlockSpec((None, tb, bw), lambda bi, gi, ti: (bi, ti, gi))
    args = [a.reshape(b, seq, d) for a in (r, lw, k, v, aa, bb)]
    y = pl.pallas_call(
        functools.partial(_rwkv_scan_kernel, chunk=L, gw=gw),
        grid=(b, d // bw, seq // tb),
        in_specs=[blk] * 6,
        out_specs=blk,
        out_shape=jax.ShapeDtypeStruct((b, seq, d), F32),
        scratch_shapes=[pltpu.VMEM((ns, gw, gw), F32)],
        compiler_params=_params(("parallel", "parallel", "arbitrary"), 14 * tb * bw * 4 + 60 * ns * gw * gw * 4),
        name="rwkv_scan",
    )(*args)
    return y.reshape(n, d)


def _rwkv_post_kernel(y_ref, r_ref, k_ref, v_ref, g_ref, rk_ref, lnw_ref, lnb_ref, o_ref):
    gw = y_ref.shape[1]
    ones = _seg_ones(gw, RWKV_HEAD_DIM)
    inv = 1.0 / RWKV_HEAD_DIM
    y = y_ref[...]
    yc = y - _segsum(y, ones) * inv
    var = _segsum(yc * yc, ones) * inv
    yn = yc * lax.rsqrt(var + RWKV_GN_EPS) * lnw_ref[...] + lnb_ref[...]
    bonus = _segsum(r_ref[...] * k_ref[...] * rk_ref[...], ones) * v_ref[...]
    o_ref[...] = ((yn + bonus) * g_ref[...]).astype(o_ref.dtype)


def rwkv_post(y, r, k, v, g, r_k, ln_w, ln_b):
    n, d = y.shape
    gw = _tile(d, RWKV_G * RWKV_HEAD_DIM)
    tr = _tile(n, 2 * ROW_TILE, 8)
    blk = pl.BlockSpec((tr, gw), lambda i, j: (i, j))
    vec = pl.BlockSpec((1, gw), lambda i, j: (0, j))
    return pl.pallas_call(
        _rwkv_post_kernel,
        grid=(n // tr, d // gw),
        in_specs=[blk] * 5 + [vec] * 3,
        out_specs=blk,
        out_shape=jax.ShapeDtypeStruct((n, d), BF16),
        compiler_params=_params(("parallel", "parallel"), 30 * tr * gw * 4),
        name="rwkv_post",
    )(y, r, k, v, g, *[p.reshape(1, d).astype(F32) for p in (r_k, ln_w, ln_b)])


def _router_kernel(h_ref, g_ref, w_ref, o_ref, *, n_exp):
    xh, xm = _split2(_rms(h_ref[...], g_ref[...]))
    wh, wm = w_ref[0], w_ref[1]
    logits = _dot(xh, wh) + _dot(xh, wm) + _dot(xm, wh)
    lane = lax.broadcasted_iota(jnp.int32, logits.shape, 1)
    valid = lane < n_exp
    logits = jnp.where(valid, logits, NEG_BIG)
    e = jnp.exp(logits - jnp.max(logits, axis=1, keepdims=True))
    probs = jnp.where(valid, e / jnp.sum(e, axis=1, keepdims=True), -1.0)
    big = logits.shape[1]
    p1 = jnp.max(probs, axis=1, keepdims=True)
    i1 = jnp.min(jnp.where(probs == p1, lane, big), axis=1, keepdims=True)
    rest = jnp.where(lane == i1, -1.0, probs)
    p2 = jnp.max(rest, axis=1, keepdims=True)
    i2 = jnp.min(jnp.where(rest == p2, lane, big), axis=1, keepdims=True)
    den = p1 + p2
    o_ref[...] = jnp.where(lane == i1, p1 / den, 0.0) + jnp.where(lane == i2, p2 / den, 0.0)


def moe_gates(h, gain, router):
    n, d = h.shape
    n_exp = router.shape[1]
    assert n_exp <= V7X_LANES and TOP_K == 2
    w = jnp.pad(router.astype(F32), ((0, 0), (0, V7X_LANES - n_exp)))
    wh = w.astype(BF16)
    w2 = jnp.stack([wh, (w - wh.astype(F32)).astype(BF16)])
    tr = _tile(n, ROW_TILE, 8)
    return pl.pallas_call(
        functools.partial(_router_kernel, n_exp=n_exp),
        grid=(n // tr,),
        in_specs=[pl.BlockSpec((tr, d), lambda i: (i, 0)),
                  pl.BlockSpec((1, d), lambda i: (0, 0)),
                  pl.BlockSpec((2, d, V7X_LANES), lambda i: (0, 0, 0))],
        out_specs=pl.BlockSpec((tr, V7X_LANES), lambda i: (i, 0)),
        out_shape=jax.ShapeDtypeStruct((n, V7X_LANES), F32),
        compiler_params=_params(("parallel",), 2 * tr * d * 4 + 4 * tr * d * 4 + 4 * d * V7X_LANES * 2),
        name="moe_gates",
    )(h, gain.reshape(1, d), w2)


def _epi_tanh(acc):
    return jnp.tanh(acc)


def _epi_sigmoid(acc):
    return 1.0 / (1.0 + jnp.exp(-acc))


def _epi_bias_sigmoid(acc, bias):
    return 1.0 / (1.0 + jnp.exp(-(acc + bias)))


def _epi_log_decay(acc, w0):
    x = -(acc + w0)
    softplus = jnp.maximum(x, 0.0) + jnp.log1p(jnp.exp(-jnp.abs(x)))
    return -jnp.exp(-softplus - 0.5)


def _bf(w):
    return w.astype(BF16)


def sb_layer(h, gain, w_qkv, w_o, batch):
    n, d = h.shape
    xn = rmsnorm(h, gain)
    qkv = matmul(xn, _bf(w_qkv), BF16, name="sb_qkv")
    o = sb_attention(qkv.reshape(batch, n // batch, 3 * d), SB_HEADS)
    return matmul(o.reshape(n, d), _bf(w_o), F32, epi=_epi_residual, fulls=(h,), tn=MM_TN // 2, name="sb_out")


def ca_layer(h, gain, w_qkv, q_norm, k_norm, rel_bias, w_o, batch):
    n, d = h.shape
    xn = rmsnorm(h, gain)
    qkv = matmul(xn, _bf(w_qkv), BF16, name="ca_qkv")
    o = ca_attention(qkv.reshape(batch, n // batch, 3 * d), q_norm, k_norm, rel_bias, CA_HEADS)
    return matmul(o.reshape(n, d), _bf(w_o), F32, epi=_epi_residual, fulls=(h,), tn=MM_TN // 2, name="ca_out")


def rwkv_layer(h, gain, mu, w_rkv, w0, w1, w2, a0, a1, a2, g1, g2, k_k, k_a, r_k, ln_w, ln_b, w_o, batch):
    n, d = h.shape
    seq = n // batch
    m_r, m_w, m_k, m_v, m_a, m_g = rmsnorm_shift(h, gain, mu, seq)
    r = matmul(m_r, _bf(w_rkv[0]), F32, name="rwkv_r")
    k = matmul(m_k, _bf(w_rkv[1]), F32, name="rwkv_k")
    v = matmul(m_v, _bf(w_rkv[2]), F32, name="rwkv_v")
    lw = matmul(matmul(m_w, _bf(w1), BF16, epi=_epi_tanh, name="rwkv_w1"), _bf(w2), F32,
                epi=_epi_log_decay, cols=(w0,), name="rwkv_w2")
    al = matmul(matmul(m_a, _bf(a1), BF16, name="rwkv_a1"), _bf(a2), F32,
                epi=_epi_bias_sigmoid, cols=(a0,), name="rwkv_a2")
    g = matmul(matmul(m_g, _bf(g1), BF16, epi=_epi_sigmoid, name="rwkv_g1"), _bf(g2), F32, name="rwkv_g2")
    kmod, aa, bb = rwkv_pre(k, al, k_k, k_a)
    y = rwkv_scan(r, lw, kmod, v, aa, bb, seq)
    out = rwkv_post(y, r, kmod, v, g, r_k, ln_w, ln_b)
    return matmul(out, _bf(w_o), F32, epi=_epi_residual, fulls=(h,), tn=MM_TN // 2, name="rwkv_out")


def dense_ffn(h, gain, w_gate_up, w_down):
    xn = rmsnorm(h, gain)
    mid = glu_matmul(xn, _bf(w_gate_up)[None], name="ffn_up")
    return matmul(mid, _bf(w_down), F32, epi=_epi_residual, fulls=(h,), name="ffn_down")


def moe_ffn(h, gain, router, w_gate_up, w_down):
    n_exp, f, d = w_down.shape
    xn = rmsnorm(h, gain)
    gates = moe_gates(h, gain, router)
    mid = glu_matmul(xn, _bf(w_gate_up), gates, name="moe_up")
    return matmul(mid, _bf(w_down).reshape(n_exp * f, d), F32, epi=_epi_residual, fulls=(h,), name="moe_down")


def kernel(x, mix_norm, ffn_norm, sb_w_qkv, sb_w_o, rwkv_mu, rwkv_w_rkv, rwkv_w0, rwkv_w1, rwkv_w2, rwkv_a0, rwkv_a1, rwkv_a2, rwkv_g1, rwkv_g2, rwkv_k_k, rwkv_k_a, rwkv_r_k, rwkv_ln_w, rwkv_ln_b, rwkv_w_o, ca_w_qkv, ca_q_norm, ca_k_norm, ca_rel_bias, ca_w_o, ffn_w_gate_up, ffn_w_down, moe_router, moe_w_gate_up, moe_w_down):
    batch, seq, d = x.shape
    h = x.reshape(batch * seq, d)
    for i in range(mix_norm.shape[0]):
        kind, j = i % N_MIXERS, i // N_MIXERS
        if kind == 0:
            h = sb_layer(h, mix_norm[i], sb_w_qkv[j], sb_w_o[j], batch)
        elif kind == 1:
            h = rwkv_layer(h, mix_norm[i], rwkv_mu[j], rwkv_w_rkv[j], rwkv_w0[j], rwkv_w1[j], rwkv_w2[j],
                           rwkv_a0[j], rwkv_a1[j], rwkv_a2[j], rwkv_g1[j], rwkv_g2[j], rwkv_k_k[j],
                           rwkv_k_a[j], rwkv_r_k[j].reshape(-1), rwkv_ln_w[j], rwkv_ln_b[j], rwkv_w_o[j], batch)
        else:
            h = ca_layer(h, mix_norm[i], ca_w_qkv[j], ca_q_norm[j], ca_k_norm[j], ca_rel_bias[j], ca_w_o[j], batch)
        f = i // 2
        if i % 2 == 0:
            h = dense_ffn(h, ffn_norm[i], ffn_w_gate_up[f], ffn_w_down[f])
        else:
            h = moe_ffn(h, ffn_norm[i], moe_router[f], moe_w_gate_up[f], moe_w_down[f])
    return h.reshape(batch, seq, d)
```

```python
import functools

import jax
import jax.numpy as jnp
from jax import lax
from jax.experimental import pallas as pl
from jax.experimental.pallas import tpu as pltpu

F32 = jnp.float32
BF16 = jnp.bfloat16

NORM_EPS = 1e-6
SB_HEADS = 32
RWKV_HEAD_DIM = 64
RWKV_GN_EPS = 64e-5
CA_HEADS = 32
CHUNK = 64
LEFT_CHUNKS = 8
REL_CLIP = 256
TOP_K = 2
N_MIXERS = 3

V7X_LANES = 128
V7X_MXU_DIM = 256
V7X_VMEM_BYTES = 64 * 1024 * 1024
V7X_VMEM_CAP = 56 * 1024 * 1024

MM_TM = 1024
MM_TN = 1024
MM_TK = 2048
ROW_TILE = 256
MOE_TM = 512
SB_TQ = 512
SB_TK = 256
SB_HEADS_PER_STEP = 4
CA_TILE = 256
RWKV_L = 64
RWKV_G = 4
RWKV_TB = 256
RWKV_STREAMS = 4
SCAN_SPLIT_MAIN = True
SCAN_SPLIT_INV = False


def _tile(dim, pref, align=V7X_LANES):
    if dim <= pref:
        return dim
    t = (pref // align) * align
    while t >= align:
        if dim % t == 0:
            return t
        t -= align
    return dim


def _params(sem, vmem_bytes):
    limit = int(min(max(vmem_bytes * 5 // 4 + (4 << 20), 16 << 20), V7X_VMEM_CAP))
    return pltpu.CompilerParams(dimension_semantics=sem, vmem_limit_bytes=limit)


def _rms(x, gain):
    return x * lax.rsqrt(jnp.mean(x * x, axis=-1, keepdims=True) + NORM_EPS) * gain


def _rmsnorm_kernel(h_ref, g_ref, o_ref):
    o_ref[...] = _rms(h_ref[...], g_ref[...]).astype(o_ref.dtype)


def rmsnorm(h, gain, out_dtype=BF16):
    n, d = h.shape
    tr = _tile(n, ROW_TILE, 8)
    return pl.pallas_call(
        _rmsnorm_kernel,
        grid=(n // tr,),
        in_specs=[pl.BlockSpec((tr, d), lambda i: (i, 0)),
                  pl.BlockSpec((1, d), lambda i: (0, 0))],
        out_specs=pl.BlockSpec((tr, d), lambda i: (i, 0)),
        out_shape=jax.ShapeDtypeStruct((n, d), out_dtype),
        compiler_params=_params(("parallel",), 2 * tr * d * 6),
        name="rmsnorm",
    )(h, gain.reshape(1, d))


def _rmsnorm_shift_kernel(h_ref, hp_ref, g_ref, mu_ref, *o_refs, tiles_per_seq):
    i = pl.program_id(0)
    g = g_ref[...]
    xn = _rms(h_ref[...], g)
    prev_last = _rms(hp_ref[...], g)[7:8, :]
    prev_last = jnp.where(i % tiles_per_seq == 0, 0.0, prev_last)
    rows = lax.broadcasted_iota(jnp.int32, xn.shape, 0)
    shifted = jnp.where(rows == 0, prev_last, pltpu.roll(xn, 1, axis=0))
    dx = shifted - xn
    for m, o_ref in enumerate(o_refs):
        o_ref[...] = (xn + dx * mu_ref[m:m + 1, :]).astype(o_ref.dtype)


def rmsnorm_shift(h, gain, mu, seq):
    n, d = h.shape
    nm = mu.shape[0]
    tr = _tile(seq, ROW_TILE, 8)
    tps = seq // tr
    sub = tr // 8
    return pl.pallas_call(
        functools.partial(_rmsnorm_shift_kernel, tiles_per_seq=tps),
        grid=(n // tr,),
        in_specs=[pl.BlockSpec((tr, d), lambda i: (i, 0)),
                  pl.BlockSpec((8, d), lambda i: (jnp.maximum(i * sub - 1, 0), 0)),
                  pl.BlockSpec((1, d), lambda i: (0, 0)),
                  pl.BlockSpec((nm, d), lambda i: (0, 0))],
        out_specs=[pl.BlockSpec((tr, d), lambda i: (i, 0))] * nm,
        out_shape=[jax.ShapeDtypeStruct((n, d), BF16)] * nm,
        compiler_params=_params(("parallel",), 2 * tr * d * (4 + 2 * nm) + 4 * tr * d * 4),
        name="rmsnorm_shift",
    )(h, h, gain.reshape(1, d), mu)


def _mm_kernel(*refs, n_extra, epi, nk):
    a_ref, b_ref = refs[0], refs[1]
    extra = refs[2:2 + n_extra]
    o_ref = refs[2 + n_extra]

    def finish(acc):
        o_ref[...] = epi(acc, *[e[...] for e in extra]).astype(o_ref.dtype)

    part = jnp.dot(a_ref[...], b_ref[...], preferred_element_type=F32)
    if nk == 1:
        finish(part)
        return
    acc_ref = refs[3 + n_extra]
    k = pl.program_id(2)

    @pl.when(k == 0)
    def _():
        acc_ref[...] = part

    @pl.when(jnp.logical_and(k > 0, k < nk - 1))
    def _():
        acc_ref[...] += part

    @pl.when(k == nk - 1)
    def _():
        finish(acc_ref[...] + part)


def _epi_id(acc):
    return acc


def matmul(a, b, out_dtype, epi=_epi_id, cols=(), fulls=(), tm=MM_TM, tn=MM_TN, tk=MM_TK, name="matmul"):
    m, kd = a.shape
    _, n = b.shape
    tm, tn = _tile(m, tm, 8), _tile(n, tn)
    tk = kd if kd <= 2 * tk else _tile(kd, tk)
    nk = kd // tk
    extras = [c.reshape(1, n).astype(F32) for c in cols] + list(fulls)
    in_specs = [pl.BlockSpec((tm, tk), lambda i, j, k: (i, k)),
                pl.BlockSpec((tk, tn), lambda i, j, k: (k, j))]
    in_specs += [pl.BlockSpec((1, tn), lambda i, j, k: (0, j)) for _ in cols]
    in_specs += [pl.BlockSpec((tm, tn), lambda i, j, k: (i, j)) for _ in fulls]
    osize = jnp.dtype(out_dtype).itemsize
    vmem = 2 * (tm * tk + tk * tn) * a.dtype.itemsize + 2 * tm * tn * osize
    vmem += sum(2 * tm * tn * f.dtype.itemsize for f in fulls) + (tm * tn * 4 if nk > 1 else 0)
    vmem += tm * tn * 4
    return pl.pallas_call(
        functools.partial(_mm_kernel, n_extra=len(extras), epi=epi, nk=nk),
        grid=(m // tm, n // tn, nk),
        in_specs=in_specs,
        out_specs=pl.BlockSpec((tm, tn), lambda i, j, k: (i, j)),
        out_shape=jax.ShapeDtypeStruct((m, n), out_dtype),
        scratch_shapes=[pltpu.VMEM((tm, tn), F32)] if nk > 1 else [],
        compiler_params=_params(("parallel", "parallel", "arbitrary"), vmem),
        name=name,
    )(a, b, *extras)


def _epi_residual(acc, res):
    return res + acc


def _silu(x):
    return x * (1.0 / (1.0 + jnp.exp(-x)))


def _glu_kernel(a_ref, bg_ref, bu_ref, o_ref):
    a = a_ref[...]
    o_ref[...] = (_silu(_dot(a, bg_ref[...])) * _dot(a, bu_ref[...])).astype(o_ref.dtype)


def glu_matmul(a, w, tm=MM_TM, tn=MM_TN // 2, name="glu_matmul"):
    m, kd = a.shape
    f = w.shape[1] // 2
    tm = _tile(m, tm, 8)
    tn = _tile(f, tn)
    nj = f // tn
    vmem = 2 * (tm * kd + 2 * kd * tn) * 2 + 2 * tm * tn * 2 + 4 * tm * tn * 4
    return pl.pallas_call(
        _glu_kernel,
        grid=(m // tm, nj),
        in_specs=[pl.BlockSpec((tm, kd), lambda i, j: (i, 0)),
                  pl.BlockSpec((kd, tn), lambda i, j: (0, j)),
                  pl.BlockSpec((kd, tn), lambda i, j: (0, nj + j))],
        out_specs=pl.BlockSpec((tm, tn), lambda i, j: (i, j)),
        out_shape=jax.ShapeDtypeStruct((m, f), BF16),
        compiler_params=_params(("parallel", "parallel"), vmem),
        name=name,
    )(a, w, w)


def _split2(x):
    hi = x.astype(BF16)
    return hi, (x - hi.astype(F32)).astype(BF16)


def _split3(x):
    hi = x.astype(BF16)
    r1 = x - hi.astype(F32)
    mid = r1.astype(BF16)
    return hi, mid, (r1 - mid.astype(F32)).astype(BF16)


def _dot(a, b):
    return jnp.dot(a, b, preferred_element_type=F32)


def _dot_nt(a, b):
    return lax.dot_general(a, b, (((1,), (1,)), ((), ())), preferred_element_type=F32)


def _dot_tn(a, b):
    return lax.dot_general(a, b, (((0,), (0,)), ((), ())), preferred_element_type=F32)


def _sb_kernel(q_ref, k_ref, v_ref, o_ref, *, scale, hd):
    tq = q_ref.shape[0]
    tk = min(SB_TK, tq)
    per_q = tq // tk
    n_h = q_ref.shape[1] // hd
    qi = pl.program_id(2)
    later = (lax.broadcasted_iota(jnp.int32, (tk, tk), 0)
             > lax.broadcasted_iota(jnp.int32, (tk, tk), 1)).astype(BF16)
    row = lax.broadcasted_iota(jnp.int32, (tq, tk), 0)
    col = lax.broadcasted_iota(jnp.int32, (tq, tk), 1)

    def block(kb, carry, past):
        start = pl.multiple_of(kb * tk, tk)
        lanes = [slice(h * hd, (h + 1) * hd) for h in range(n_h)]
        z, ls, parts, c_new, suf, att, o_new = {}, {}, {}, {}, {}, {}, {}

        def scores(h):
            z[h] = _dot_nt(q_ref[:, lanes[h]], k_ref[pl.ds(start, tk), lanes[h]]) * scale

        def log_terms(h):
            ls[h] = jnp.minimum(z[h], 0.0) - jnp.log(1.0 + jnp.exp(-jnp.abs(z[h])))
            lk = ls[h] - z[h]
            if past is not None:
                lk = jnp.where(past, lk, 0.0)
            parts[h] = lk.astype(BF16)
            c_new[h] = carry[h][1] + jnp.sum(lk, axis=1, keepdims=True)

        def suffix(h):
            suf[h] = _dot(parts[h], later)

        def weights(h):
            a = jnp.exp(ls[h] + suf[h] + carry[h][1])
            att[h] = (a if past is None else jnp.where(past, a, 0.0)).astype(BF16)

        def values(h):
            o_new[h] = carry[h][0] + _dot(att[h], v_ref[pl.ds(start, tk), lanes[h]])

        stages = (scores, log_terms, suffix, weights, values)
        for step in range(n_h + len(stages) - 1):
            for s in reversed(range(len(stages))):
                if 0 <= step - s < n_h:
                    stages[s](step - s)
        return tuple((o_new[h], c_new[h]) for h in range(n_h))

    carry = tuple((jnp.zeros((tq, hd), F32), jnp.zeros((tq, 1), F32)) for _ in range(n_h))
    for m in range(per_q):
        carry = block(qi * per_q + per_q - 1 - m, carry, col + (per_q - 1 - m) * tk < row)
    n_full = qi * per_q
    carry = lax.fori_loop(0, n_full, lambda it, cr: block(n_full - 1 - it, cr, None), carry)
    for h in range(n_h):
        o_ref[:, h * hd:(h + 1) * hd] = carry[h][0].astype(o_ref.dtype)


def sb_attention(qkv, n_heads):
    b, t, d3 = qkv.shape
    d = d3 // 3
    hd = d // n_heads
    tq = _tile(t, SB_TQ, SB_TK)
    hps = SB_HEADS_PER_STEP if n_heads % SB_HEADS_PER_STEP == 0 else 1
    w = hps * hd
    ng = n_heads // hps
    return pl.pallas_call(
        functools.partial(_sb_kernel, scale=hd ** -0.5, hd=hd),
        grid=(b, ng, t // tq),
        in_specs=[pl.BlockSpec((None, tq, w), lambda bi, g, qi: (bi, qi, g)),
                  pl.BlockSpec((None, t, w), lambda bi, g, qi: (bi, 0, ng + g)),
                  pl.BlockSpec((None, t, w), lambda bi, g, qi: (bi, 0, 2 * ng + g))],
        out_specs=pl.BlockSpec((None, tq, w), lambda bi, g, qi: (bi, qi, g)),
        out_shape=jax.ShapeDtypeStruct((b, t, d), BF16),
        compiler_params=_params(("parallel", "parallel", "arbitrary"),
                                4 * t * w * 2 + 4 * tq * w * 2 + 12 * hps * tq * SB_TK * 4),
        name="sb_attention",
    )(qkv, qkv, qkv)


NEG_BIG = -1e30


def _ca_kernel(q_ref, k_ref, v_ref, qg_ref, kg_ref, bias_ref, o_ref, kn_ref, *, scale, n_win):
    t = q_ref.shape[0]
    qi = pl.program_id(2)

    @pl.when(qi == 0)
    def _():
        kn_ref[...] = _rms(k_ref[...].astype(F32), kg_ref[...]).astype(BF16)

    qn = _rms(q_ref[...].astype(F32), qg_ref[...]).astype(BF16)
    scores, starts = [], []
    for w in range(n_win):
        kb = qi - (n_win - 1) + w
        start = pl.multiple_of(jnp.maximum(kb, 0) * t, t)
        s = _dot_nt(qn, kn_ref[pl.ds(start, t), :]) * scale + bias_ref[:, w * t:(w + 1) * t]
        scores.append(jnp.where(kb >= 0, s, NEG_BIG))
        starts.append(start)
    m = functools.reduce(jnp.maximum, [jnp.max(s, axis=1, keepdims=True) for s in scores])
    acc = jnp.zeros(o_ref.shape, F32)
    den = jnp.zeros((t, 1), F32)
    for s, start in zip(scores, starts):
        p = jnp.exp(s - m)
        den = den + jnp.sum(p, axis=1, keepdims=True)
        acc = acc + _dot(p.astype(BF16), v_ref[pl.ds(start, t), :])
    o_ref[...] = (acc / den).astype(o_ref.dtype)


def _ca_bias_table(rel_bias, t):
    left = LEFT_CHUNKS * CHUNK
    win = left + t
    i = jnp.arange(t)[:, None]
    j = jnp.arange(win)[None, :]
    qc = (i + left) // CHUNK
    kc = j // CHUNK
    allowed = jnp.logical_and(kc <= qc, kc >= qc - LEFT_CHUNKS)
    n = win + t
    dist = jnp.clip(left + t - 1 - jnp.arange(n), -REL_CLIP, REL_CLIP) + REL_CLIP
    u = rel_bias.astype(F32)[:, dist]
    shifted = jnp.tile(u, (1, t))[:, :t * (n - 1)].reshape(-1, t, n - 1)
    return jnp.where(allowed[None], shifted[:, :, t - 1:t - 1 + win], NEG_BIG)


def ca_attention(qkv, q_gain, k_gain, rel_bias, n_heads):
    b, t, d3 = qkv.shape
    d = d3 // 3
    hd = d // n_heads
    tq = _tile(t, CA_TILE, CHUNK)
    left = LEFT_CHUNKS * CHUNK
    assert left % tq == 0 and tq % CHUNK == 0
    n_win = left // tq + 1
    bias = _ca_bias_table(rel_bias, tq)
    return pl.pallas_call(
        functools.partial(_ca_kernel, scale=hd ** -0.5, n_win=n_win),
        grid=(b, n_heads, t // tq),
        in_specs=[pl.BlockSpec((None, tq, hd), lambda bi, h, qi: (bi, qi, h)),
                  pl.BlockSpec((None, t, hd), lambda bi, h, qi: (bi, 0, n_heads + h)),
                  pl.BlockSpec((None, t, hd), lambda bi, h, qi: (bi, 0, 2 * n_heads + h)),
                  pl.BlockSpec((1, hd), lambda bi, h, qi: (0, 0)),
                  pl.BlockSpec((1, hd), lambda bi, h, qi: (0, 0)),
                  pl.BlockSpec((None, tq, left + tq), lambda bi, h, qi: (h, 0, 0))],
        out_specs=pl.BlockSpec((None, tq, hd), lambda bi, h, qi: (bi, qi, h)),
        out_shape=jax.ShapeDtypeStruct((b, t, d), BF16),
        scratch_shapes=[pltpu.VMEM((t, hd), BF16)],
        compiler_params=_params(("parallel", "parallel", "arbitrary"),
                                5 * t * hd * 2 + 4 * tq * hd * 2 + 2 * tq * (left + tq) * 4
                                + 8 * tq * (left + tq) * 4 + t * hd * 8),
        name="ca_attention",
    )(qkv, qkv, qkv, q_gain.reshape(1, hd).astype(F32), k_gain.reshape(1, hd).astype(F32), bias)


def _seg_ones(n, seg):
    r = lax.broadcasted_iota(jnp.int32, (n, n), 0) // seg
    c = lax.broadcasted_iota(jnp.int32, (n, n), 1) // seg
    return (r == c).astype(BF16)


def _segsum(x, ones):
    hi, mid, lo = _split3(x)
    return _dot(hi, ones) + _dot(mid, ones) + _dot(lo, ones)


def _rwkv_pre_kernel(k_ref, a_ref, kk_ref, ka_ref, kmod_ref, aa_ref, bb_ref):
    k = k_ref[...]
    al = a_ref[...]
    ones = _seg_ones(k.shape[1], RWKV_HEAD_DIM)
    kk = k * kk_ref[...]
    kk = kk * lax.rsqrt(jnp.maximum(_segsum(kk * kk, ones), 1e-24))
    kmod_ref[...] = k * (1.0 + (al - 1.0) * ka_ref[...])
    aa_ref[...] = -kk
    bb_ref[...] = kk * al


def rwkv_pre(k, al, k_k, k_a):
    n, d = k.shape
    gw = _tile(d, RWKV_G * RWKV_HEAD_DIM)
    tr = _tile(n, 2 * ROW_TILE, 8)
    blk = pl.BlockSpec((tr, gw), lambda i, j: (i, j))
    vec = pl.BlockSpec((1, gw), lambda i, j: (0, j))
    shp = jax.ShapeDtypeStruct((n, d), F32)
    return pl.pallas_call(
        _rwkv_pre_kernel,
        grid=(n // tr, d // gw),
        in_specs=[blk, blk, vec, vec],
        out_specs=[blk, blk, blk],
        out_shape=[shp, shp, shp],
        compiler_params=_params(("parallel", "parallel"), 20 * tr * gw * 4),
        name="rwkv_pre",
    )(k, al, k_k.reshape(1, d).astype(F32), k_a.reshape(1, d).astype(F32))


def _rwkv_scan_kernel(r_ref, lw_ref, k_ref, v_ref, aa_ref, bb_ref, y_ref, mt_ref, *, chunk, gw):
    tb = r_ref.shape[0]
    L = chunk
    G = gw // L
    streams = range(r_ref.shape[1] // gw)
    chunks = range(tb // L)
    ti = pl.program_id(2)

    @pl.when(ti == 0)
    def _():
        mt_ref[...] = jnp.zeros_like(mt_ref)

    bd = (lax.broadcasted_iota(jnp.int32, (gw, gw), 0) // L) == (lax.broadcasted_iota(jnp.int32, (gw, gw), 1) // L)
    row = lax.broadcasted_iota(jnp.int32, (L, gw), 0)
    col = lax.broadcasted_iota(jnp.int32, (L, gw), 1) % L
    tri_incl = col <= row
    tri_strict = col < row
    eye = (col == row).astype(F32)
    ltri = (lax.broadcasted_iota(jnp.int32, (L, L), 1) <= lax.broadcasted_iota(jnp.int32, (L, L), 0)).astype(BF16)
    zero_b = jnp.zeros((), BF16)

    def stack(xb):
        return jnp.where(bd, jnp.concatenate([xb] * G, axis=0), zero_b)

    def parts(y, two):
        return _split2(y) if two else (y.astype(BF16),)

    def mm(dot, x, ws):
        xh, xm = _split2(x)
        out = dot(xh, ws[0]) + dot(xm, ws[0])
        return out + dot(xh, ws[1]) if len(ws) > 1 else out

    def prod(x, y, two):
        return mm(_dot, x, [stack(p) for p in parts(y, two)])

    def prod_nt(x, y, two):
        return mm(_dot_nt, x, [stack(p) for p in parts(y, two)])

    def rows(*xs):
        return jnp.concatenate(xs, axis=0)

    units = [(c, s) for c in chunks for s in streams]
    loc = {}
    for c, s in units:
        sl = (pl.ds(c * L, L), slice(s * gw, (s + 1) * gw))
        r, lw, k, v, aa, bb = (ref[sl] for ref in (r_ref, lw_ref, k_ref, v_ref, aa_ref, bb_ref))
        cw = functools.reduce(lambda acc, part: acc + _dot(ltri, part), _split3(lw), jnp.zeros((L, gw), F32))
        cmid = cw[L // 2 - 1:L // 2, :]
        ctot = cw[L - 1:L, :]
        at_state = aa * jnp.exp(cw - lw)
        rt_state = r * jnp.exp(cw)
        to_mid = jnp.exp(-cmid)
        e_mid = jnp.exp(cmid - cw)
        e_tot = jnp.exp(ctot - cw)
        lhs = rows(at_state * to_mid, rt_state * to_mid)
        pb = prod_nt(lhs, bb * e_mid, SCAN_SPLIT_MAIN)
        pk = prod_nt(lhs, k * e_mid, SCAN_SPLIT_MAIN)
        a_ak = jnp.where(tri_strict, pk[:L], 0.0)
        a_rk = jnp.where(tri_incl, pk[L:], 0.0)
        av = prod(rows(a_ak, a_rk), v, SCAN_SPLIT_MAIN)
        loc[c, s] = dict(sl=sl, v=v, st=rows(at_state, rt_state), n=jnp.where(tri_strict, pb[:L], 0.0),
                         a_rb=jnp.where(tri_incl, pb[L:], 0.0), akv=av[:L], rkv=av[L:],
                         wk=rows(bb * e_tot, k * e_tot), decay=jnp.exp(ctot))

    n_lev = L.bit_length() - 2
    pw = {u: prod(loc[u]["n"], loc[u]["n"], SCAN_SPLIT_INV) for u in units}
    tm = {u: eye + loc[u]["n"] for u in units}
    for lev in range(n_lev):
        if lev < n_lev - 1:
            res = {u: prod(rows(tm[u], pw[u]), pw[u], SCAN_SPLIT_INV) for u in units}
            tm = {u: tm[u] + res[u][:L] for u in units}
            pw = {u: res[u][L:] for u in units}
        else:
            tm = {u: tm[u] + prod(tm[u], pw[u], SCAN_SPLIT_INV) for u in units}
    tr = {u: rows(tm[u], prod(loc[u]["a_rb"], tm[u], SCAN_SPLIT_MAIN)) for u in units}

    mt = [mt_ref[s] for s in streams]
    for c in chunks:
        xr = [mm(_dot_nt, loc[c, s]["st"], parts(mt[s], SCAN_SPLIT_MAIN)) for s in streams]
        ur = [prod(tr[c, s], xr[s][:L] + loc[c, s]["akv"], SCAN_SPLIT_MAIN) for s in streams]
        for s in streams:
            y_ref[loc[c, s]["sl"]] = xr[s][L:] + ur[s][L:] + loc[c, s]["rkv"]
        upd = [mm(_dot_tn, rows(ur[s][:L], loc[c, s]["v"]), parts(loc[c, s]["wk"], SCAN_SPLIT_MAIN)) for s in streams]
        mt = [mt[s] * loc[c, s]["decay"] + jnp.where(bd, upd[s], 0.0) for s in streams]
    for s in streams:
        mt_ref[s] = mt[s]


def rwkv_scan(r, lw, k, v, aa, bb, seq):
    n, d = r.shape
    b = n // seq
    L = RWKV_L
    assert L == RWKV_HEAD_DIM and L & (L - 1) == 0
    gw = _tile(d, RWKV_G * RWKV_HEAD_DIM)
    ns = RWKV_STREAMS if d % (RWKV_STREAMS * gw) == 0 else 1
    bw = ns * gw
    tb = _tile(seq, RWKV_TB, L)
    blk = pl.BlockSpec((None, tb, bw), lambda bi, gi, ti: (bi, ti, gi))
    args = [a.reshape(b, seq, d) for a in (r, lw, k, v, aa, bb)]
    y = pl.pallas_call(
        functools.partial(_rwkv_scan_kernel, chunk=L, gw=gw),
        grid=(b, d // bw, seq // tb),
        in_specs=[blk] * 6,
        out_specs=blk,
        out_shape=jax.ShapeDtypeStruct((b, seq, d), F32),
        scratch_shapes=[pltpu.VMEM((ns, gw, gw), F32)],
        compiler_params=_params(("parallel", "parallel", "arbitrary"), 14 * tb * bw * 4 + 60 * ns * gw * gw * 4),
        name="rwkv_scan",
    )(*args)
    return y.reshape(n, d)


def _rwkv_post_kernel(y_ref, r_ref, k_ref, v_ref, g_ref, rk_ref, lnw_ref, lnb_ref, o_ref):
    gw = y_ref.shape[1]
    ones = _seg_ones(gw, RWKV_HEAD_DIM)
    inv = 1.0 / RWKV_HEAD_DIM
    y = y_ref[...]
    yc = y - _segsum(y, ones) * inv
    var = _segsum(yc * yc, ones) * inv
    yn = yc * lax.rsqrt(var + RWKV_GN_EPS) * lnw_ref[...] + lnb_ref[...]
    bonus = _segsum(r_ref[...] * k_ref[...] * rk_ref[...], ones) * v_ref[...]
    o_ref[...] = ((yn + bonus) * g_ref[...]).astype(o_ref.dtype)


def rwkv_post(y, r, k, v, g, r_k, ln_w, ln_b):
    n, d = y.shape
    gw = _tile(d, RWKV_G * RWKV_HEAD_DIM)
    tr = _tile(n, 2 * ROW_TILE, 8)
    blk = pl.BlockSpec((tr, gw), lambda i, j: (i, j))
    vec = pl.BlockSpec((1, gw), lambda i, j: (0, j))
    return pl.pallas_call(
        _rwkv_post_kernel,
        grid=(n // tr, d // gw),
        in_specs=[blk] * 5 + [vec] * 3,
        out_specs=blk,
        out_shape=jax.ShapeDtypeStruct((n, d), BF16),
        compiler_params=_params(("parallel", "parallel"), 30 * tr * gw * 4),
        name="rwkv_post",
    )(y, r, k, v, g, *[p.reshape(1, d).astype(F32) for p in (r_k, ln_w, ln_b)])


def _router_kernel(h_ref, g_ref, w_ref, o_ref, xn_ref, *, n_exp):
    xn = _rms(h_ref[...], g_ref[...])
    xn_ref[...] = xn
    xh, xm = _split2(xn)
    wh, wm = w_ref[0], w_ref[1]
    logits = _dot(xh, wh) + _dot(xh, wm) + _dot(xm, wh)
    lane = lax.broadcasted_iota(jnp.int32, logits.shape, 1)
    valid = lane < n_exp
    logits = jnp.where(valid, logits, NEG_BIG)
    e = jnp.exp(logits - jnp.max(logits, axis=1, keepdims=True))
    probs = jnp.where(valid, e / jnp.sum(e, axis=1, keepdims=True), -1.0)
    big = logits.shape[1]
    p1 = jnp.max(probs, axis=1, keepdims=True)
    i1 = jnp.min(jnp.where(probs == p1, lane, big), axis=1, keepdims=True)
    rest = jnp.where(lane == i1, -1.0, probs)
    p2 = jnp.max(rest, axis=1, keepdims=True)
    i2 = jnp.min(jnp.where(rest == p2, lane, big), axis=1, keepdims=True)
    den = p1 + p2
    fields = (i1.astype(F32), i2.astype(F32), p1 / den, p2 / den)
    o_ref[...] = functools.reduce(lambda acc, kv: jnp.where(lane == kv[0], kv[1], acc), enumerate(fields),
                                  jnp.zeros(logits.shape, F32))


def moe_route(h, gain, router):
    n, d = h.shape
    n_exp = router.shape[1]
    assert n_exp <= V7X_LANES and TOP_K == 2
    w = jnp.pad(router.astype(F32), ((0, 0), (0, V7X_LANES - n_exp)))
    wh = w.astype(BF16)
    w2 = jnp.stack([wh, (w - wh.astype(F32)).astype(BF16)])
    tr = _tile(n, ROW_TILE, 8)
    return pl.pallas_call(
        functools.partial(_router_kernel, n_exp=n_exp),
        grid=(n // tr,),
        in_specs=[pl.BlockSpec((tr, d), lambda i: (i, 0)),
                  pl.BlockSpec((1, d), lambda i: (0, 0)),
                  pl.BlockSpec((2, d, V7X_LANES), lambda i: (0, 0, 0))],
        out_specs=[pl.BlockSpec((tr, V7X_LANES), lambda i: (i, 0)), pl.BlockSpec((tr, d), lambda i: (i, 0))],
        out_shape=[jax.ShapeDtypeStruct((n, V7X_LANES), F32), jax.ShapeDtypeStruct((n, d), F32)],
        compiler_params=_params(("parallel",), 4 * tr * d * 4 + 4 * tr * d * 4 + 4 * d * V7X_LANES * 2),
        name="moe_route",
    )(h, gain.reshape(1, d), w2)


def _route_plan(sel, n_exp, tm):
    n = sel.shape[0]
    expert = jnp.concatenate([sel[:, 0], sel[:, 1]]).astype(jnp.int32)
    gate = jnp.concatenate([sel[:, 2], sel[:, 3]])
    onehot = (expert[:, None] == jnp.arange(n_exp, dtype=jnp.int32)[None, :]).astype(jnp.int32)
    rank = jnp.sum((jnp.cumsum(onehot, axis=0) - onehot) * onehot, axis=1)
    padded = (jnp.sum(onehot, axis=0) + tm - 1) // tm * tm
    ends = jnp.cumsum(padded)
    dest = (ends - padded)[expert] + rank
    n_rows = TOP_K * n + n_exp * tm
    row_token = jnp.zeros((n_rows,), jnp.int32).at[dest].set(jnp.tile(jnp.arange(n, dtype=jnp.int32), TOP_K))
    row_gate = jnp.zeros((n_rows,), F32).at[dest].set(gate)
    tile_start = jnp.arange(n_rows // tm, dtype=jnp.int32) * tm
    tile_expert = jnp.minimum(jnp.searchsorted(ends, tile_start, side="right"), n_exp - 1).astype(jnp.int32)
    return row_token, row_gate, tile_expert, dest


def _gather_kernel(*refs, n_idx, has_base, n_steps):
    cur = refs[:n_idx]
    nxt = refs[n_idx:2 * n_idx]
    src = refs[2 * n_idx]
    base = refs[2 * n_idx + 1] if has_base else None
    o_ref, buf, sem = refs[-3:]
    tr = o_ref.shape[0]
    i = pl.program_id(0)
    slot = i % 2

    def row_copy(idx_ref, a, r, s):
        return pltpu.make_async_copy(src.at[pl.ds(idx_ref[0, r], 1)], buf.at[s, a, pl.ds(r, 1)], sem.at[s, a])

    def start_all(idx_refs, s):
        for a in range(n_idx):
            lax.fori_loop(0, tr, lambda r, c: (row_copy(idx_refs[a], a, r, s).start(), c)[1], 0)

    @pl.when(i == 0)
    def _():
        start_all(cur, 0)

    @pl.when(i + 1 < n_steps)
    def _():
        start_all(nxt, 1 - slot)

    for a in range(n_idx):
        lax.fori_loop(0, tr, lambda r, c: (row_copy(cur[a], a, r, slot).wait(), c)[1], 0)
    acc = buf[slot, 0]
    for a in range(1, n_idx):
        acc = acc + buf[slot, a]
    if has_base:
        acc = base[...] + acc
    o_ref[...] = acc.astype(o_ref.dtype)


def gather_rows(src, idxs, out_dtype, base=None):
    d = src.shape[1]
    r = idxs[0].shape[0]
    tr = _tile(r, ROW_TILE, 8)
    n_steps = r // tr
    n_idx = len(idxs)
    idx3 = [ix.reshape(n_steps, 1, tr) for ix in idxs]
    smem = functools.partial(pl.BlockSpec, (None, 1, tr), memory_space=pltpu.SMEM)
    in_specs = [smem(lambda i: (i, 0, 0)) for _ in idxs]
    in_specs += [smem(lambda i: (jnp.minimum(i + 1, n_steps - 1), 0, 0)) for _ in idxs]
    in_specs.append(pl.BlockSpec(memory_space=pl.ANY))
    args = idx3 + idx3 + [src]
    if base is not None:
        in_specs.append(pl.BlockSpec((tr, d), lambda i: (i, 0)))
        args.append(base)
    return pl.pallas_call(
        functools.partial(_gather_kernel, n_idx=n_idx, has_base=base is not None, n_steps=n_steps),
        grid=(n_steps,),
        in_specs=in_specs,
        out_specs=pl.BlockSpec((tr, d), lambda i: (i, 0)),
        out_shape=jax.ShapeDtypeStruct((r, d), out_dtype),
        scratch_shapes=[pltpu.VMEM((2, n_idx, tr, d), F32), pltpu.SemaphoreType.DMA((2, n_idx))],
        compiler_params=_params(("arbitrary",), (2 * n_idx + 6) * tr * d * 4),
        name="gather_rows",
    )(*args)


def _grouped_glu_kernel(te_ref, a_ref, bg_ref, bu_ref, gate_ref, o_ref):
    a = a_ref[...]
    y = _silu(_dot(a, bg_ref[...])) * _dot(a, bu_ref[...])
    gate = gate_ref[...]
    reps = o_ref.shape[1] // gate.shape[1]
    gate = jnp.concatenate([gate] * reps, axis=1) if reps > 1 else gate[:, :o_ref.shape[1]]
    o_ref[...] = (y * gate).astype(o_ref.dtype)


def grouped_glu(a, w, gate_rows, tile_expert, tm):
    r, kd = a.shape
    f = w.shape[2] // 2
    tn = _tile(f, MM_TN // 2)
    nj = f // tn
    grid_spec = pltpu.PrefetchScalarGridSpec(
        num_scalar_prefetch=1,
        grid=(nj, r // tm),
        in_specs=[pl.BlockSpec((tm, kd), lambda j, t, te: (t, 0)),
                  pl.BlockSpec((None, kd, tn), lambda j, t, te: (te[t], 0, j)),
                  pl.BlockSpec((None, kd, tn), lambda j, t, te: (te[t], 0, nj + j)),
                  pl.BlockSpec((tm, V7X_LANES), lambda j, t, te: (t, 0))],
        out_specs=pl.BlockSpec((tm, tn), lambda j, t, te: (t, j)),
    )
    vmem = 2 * tm * kd * 2 + 4 * kd * tn * 2 + 2 * tm * tn * 2 + 4 * tm * tn * 4 + 2 * tm * V7X_LANES * 4
    return pl.pallas_call(
        _grouped_glu_kernel,
        grid_spec=grid_spec,
        out_shape=jax.ShapeDtypeStruct((r, f), BF16),
        compiler_params=_params(("arbitrary", "arbitrary"), vmem),
        name="moe_up",
    )(tile_expert, a, w, w, gate_rows)


def _grouped_mm_kernel(te_ref, a_ref, b_ref, o_ref):
    o_ref[...] = _dot(a_ref[...], b_ref[...])


def grouped_matmul(a, w, tile_expert, tm):
    r, kd = a.shape
    n = w.shape[2]
    tn = _tile(n, MM_TN)
    grid_spec = pltpu.PrefetchScalarGridSpec(
        num_scalar_prefetch=1,
        grid=(n // tn, r // tm),
        in_specs=[pl.BlockSpec((tm, kd), lambda j, t, te: (t, 0)),
                  pl.BlockSpec((None, kd, tn), lambda j, t, te: (te[t], 0, j))],
        out_specs=pl.BlockSpec((tm, tn), lambda j, t, te: (t, j)),
    )
    vmem = 2 * tm * kd * 2 + 2 * kd * tn * 2 + 3 * tm * tn * 4
    return pl.pallas_call(
        _grouped_mm_kernel,
        grid_spec=grid_spec,
        out_shape=jax.ShapeDtypeStruct((r, n), F32),
        compiler_params=_params(("arbitrary", "arbitrary"), vmem),
        name="moe_down",
    )(tile_expert, a, w)


def _epi_tanh(acc):
    return jnp.tanh(acc)


def _epi_sigmoid(acc):
    return 1.0 / (1.0 + jnp.exp(-acc))


def _epi_bias_sigmoid(acc, bias):
    return 1.0 / (1.0 + jnp.exp(-(acc + bias)))


def _epi_log_decay(acc, w0):
    x = -(acc + w0)
    softplus = jnp.maximum(x, 0.0) + jnp.log1p(jnp.exp(-jnp.abs(x)))
    return -jnp.exp(-softplus - 0.5)


def _bf(w):
    return w.astype(BF16)


def sb_layer(h, gain, w_qkv, w_o, batch):
    n, d = h.shape
    xn = rmsnorm(h, gain)
    qkv = matmul(xn, _bf(w_qkv), BF16, name="sb_qkv")
    o = sb_attention(qkv.reshape(batch, n // batch, 3 * d), SB_HEADS)
    return matmul(o.reshape(n, d), _bf(w_o), F32, epi=_epi_residual, fulls=(h,), tn=MM_TN // 2, name="sb_out")


def ca_layer(h, gain, w_qkv, q_norm, k_norm, rel_bias, w_o, batch):
    n, d = h.shape
    xn = rmsnorm(h, gain)
    qkv = matmul(xn, _bf(w_qkv), BF16, name="ca_qkv")
    o = ca_attention(qkv.reshape(batch, n // batch, 3 * d), q_norm, k_norm, rel_bias, CA_HEADS)
    return matmul(o.reshape(n, d), _bf(w_o), F32, epi=_epi_residual, fulls=(h,), tn=MM_TN // 2, name="ca_out")


def rwkv_layer(h, gain, mu, w_rkv, w0, w1, w2, a0, a1, a2, g1, g2, k_k, k_a, r_k, ln_w, ln_b, w_o, batch):
    n, d = h.shape
    seq = n // batch
    m_r, m_w, m_k, m_v, m_a, m_g = rmsnorm_shift(h, gain, mu, seq)
    r = matmul(m_r, _bf(w_rkv[0]), F32, name="rwkv_r")
    k = matmul(m_k, _bf(w_rkv[1]), F32, name="rwkv_k")
    v = matmul(m_v, _bf(w_rkv[2]), F32, name="rwkv_v")
    lw = matmul(matmul(m_w, _bf(w1), BF16, epi=_epi_tanh, name="rwkv_w1"), _bf(w2), F32,
                epi=_epi_log_decay, cols=(w0,), name="rwkv_w2")
    al = matmul(matmul(m_a, _bf(a1), BF16, name="rwkv_a1"), _bf(a2), F32,
                epi=_epi_bias_sigmoid, cols=(a0,), name="rwkv_a2")
    g = matmul(matmul(m_g, _bf(g1), BF16, epi=_epi_sigmoid, name="rwkv_g1"), _bf(g2), F32, name="rwkv_g2")
    kmod, aa, bb = rwkv_pre(k, al, k_k, k_a)
    y = rwkv_scan(r, lw, kmod, v, aa, bb, seq)
    out = rwkv_post(y, r, kmod, v, g, r_k, ln_w, ln_b)
    return matmul(out, _bf(w_o), F32, epi=_epi_residual, fulls=(h,), tn=MM_TN // 2, name="rwkv_out")


def dense_ffn(h, gain, w_gate_up, w_down):
    xn = rmsnorm(h, gain)
    mid = glu_matmul(xn, _bf(w_gate_up), name="ffn_up")
    return matmul(mid, _bf(w_down), F32, epi=_epi_residual, fulls=(h,), name="ffn_down")


def moe_ffn(h, gain, router, w_gate_up, w_down):
    n = h.shape[0]
    n_exp = w_down.shape[0]
    tm = _tile(TOP_K * n, MOE_TM, 8)
    sel, xn = moe_route(h, gain, router)
    row_token, row_gate, tile_expert, dest = _route_plan(sel, n_exp, tm)
    xs = gather_rows(xn, [row_token], BF16)
    gate_rows = jnp.broadcast_to(row_gate[:, None], (row_gate.shape[0], V7X_LANES))
    mid = grouped_glu(xs, _bf(w_gate_up), gate_rows, tile_expert, tm)
    ys = grouped_matmul(mid, _bf(w_down), tile_expert, tm)
    return gather_rows(ys, [dest[:n], dest[n:]], F32, base=h)


def kernel(x, mix_norm, ffn_norm, sb_w_qkv, sb_w_o, rwkv_mu, rwkv_w_rkv, rwkv_w0, rwkv_w1, rwkv_w2, rwkv_a0, rwkv_a1, rwkv_a2, rwkv_g1, rwkv_g2, rwkv_k_k, rwkv_k_a, rwkv_r_k, rwkv_ln_w, rwkv_ln_b, rwkv_w_o, ca_w_qkv, ca_q_norm, ca_k_norm, ca_rel_bias, ca_w_o, ffn_w_gate_up, ffn_w_down, moe_router, moe_w_gate_up, moe_w_down):
    batch, seq, d = x.shape
    h = x.reshape(batch * seq, d)
    for i in range(mix_norm.shape[0]):
        kind, j = i % N_MIXERS, i // N_MIXERS
        if kind == 0:
            h = sb_layer(h, mix_norm[i], sb_w_qkv[j], sb_w_o[j], batch)
        elif kind == 1:
            h = rwkv_layer(h, mix_norm[i], rwkv_mu[j], rwkv_w_rkv[j], rwkv_w0[j], rwkv_w1[j], rwkv_w2[j],
                           rwkv_a0[j], rwkv_a1[j], rwkv_a2[j], rwkv_g1[j], rwkv_g2[j], rwkv_k_k[j],
                           rwkv_k_a[j], rwkv_r_k[j].reshape(-1), rwkv_ln_w[j], rwkv_ln_b[j], rwkv_w_o[j], batch)
        else:
            h = ca_layer(h, mix_norm[i], ca_w_qkv[j], ca_q_norm[j], ca_k_norm[j], ca_rel_bias[j], ca_w_o[j], batch)
        f = i // 2
        if i % 2 == 0:
            h = dense_ffn(h, ffn_norm[i], ffn_w_gate_up[f], ffn_w_down[f])
        else:
            h = moe_ffn(h, ffn_norm[i], moe_router[f], moe_w_gate_up[f], moe_w_down[f])
    return h.reshape(batch, seq, d)
```
